```python
import math
import jax, jax.numpy as jnp
from jax import lax
import numpy as np

D_MODEL = 4096
BATCH = 1
SEQ = 8192
DEPTH = 4

HEAD_DIM = 128
A_HEADS = (3 * D_MODEL) // (4 * HEAD_DIM)
A_WIDTH = A_HEADS * HEAD_DIM
ROPE_DIM = HEAD_DIM // 4
ROPE_THETA = 500000.0
DILATED_PATTERNS = ((128, 1), (512, 4), (2048, 16))
ATTN_BLOCK = 128
B_WIDTH = D_MODEL // 4
POOL_SIZES = (2, 4, 8, 16)
POOL_GROUP = B_WIDTH // len(POOL_SIZES)
EVEN_IN = 3 * A_WIDTH + B_WIDTH
EVEN_CAT = A_WIDTH + B_WIDTH
GM_WIDTH = D_MODEL
GM_GROUPS = 8
GM_GROUP_DIM = GM_WIDTH // GM_GROUPS
GM_CHUNK = 128
N_EXPERTS = 16
N_EXPERT_GROUPS = 4
EXPERTS_PER_GROUP = N_EXPERTS // N_EXPERT_GROUPS
TOP_K = 2
D_EXPERT = D_MODEL // 8
EPS = 1e-6

kernel_name = "hybrid_dilated_pool_gmlp_grouped_moe"


def rms_norm(x, gain):
    xf = x.astype(jnp.float32)
    y = xf * lax.rsqrt(jnp.mean(xf * xf, axis=-1, keepdims=True) + EPS)
    return (y * gain.astype(jnp.float32)).astype(x.dtype)


def apply_rope(x, positions):
    half = ROPE_DIM // 2
    inv_freq = ROPE_THETA ** (-jnp.arange(half, dtype=jnp.float32) / half)
    ang = positions.astype(jnp.float32)[..., None] * inv_freq
    cos = jnp.cos(ang)[:, :, None, :]
    sin = jnp.sin(ang)[:, :, None, :]
    xf = x.astype(jnp.float32)
    x1, x2 = xf[..., :half], xf[..., half:ROPE_DIM]
    out = jnp.concatenate([x1 * cos - x2 * sin, x2 * cos + x1 * sin, xf[..., ROPE_DIM:]], axis=-1)
    return out.astype(x.dtype)


def dilated_branch(q, k, v, window, dilation):
    b, t, h, c = q.shape
    L = t // dilation
    n_back = window // dilation
    Lp = -(-L // ATTN_BLOCK) * ATTN_BLOCK
    nb = Lp // ATTN_BLOCK

    def to_streams(a):
        a = a.reshape(b, L, dilation, h, c).transpose(0, 2, 3, 1, 4)
        a = jnp.pad(a, ((0, 0), (0, 0), (0, 0), (0, Lp - L), (0, 0)))
        return a.reshape(b, dilation, h, nb, ATTN_BLOCK, c)

    def with_prev(a):
        prev = jnp.pad(a[:, :, :, :-1], ((0, 0), (0, 0), (0, 0), (1, 0), (0, 0), (0, 0)))
        return jnp.concatenate([prev, a], axis=4)

    qs = to_streams(q)
    kw = with_prev(to_streams(k))
    vw = with_prev(to_streams(v))
    scores = jnp.einsum('bdhnqc,bdhnkc->bdhnqk', qs, kw).astype(jnp.float32) * (c ** -0.5)
    blk = jnp.arange(nb)[:, None, None] * ATTN_BLOCK
    q_pos = blk + jnp.arange(ATTN_BLOCK)[None, :, None]
    k_pos = blk - ATTN_BLOCK + jnp.arange(2 * ATTN_BLOCK)[None, None, :]
    rel = q_pos - k_pos
    valid = (rel >= 0) & (rel <= n_back) & (k_pos >= 0)
    scores = jnp.where(valid, scores, -jnp.inf)
    m = jnp.max(scores, axis=-1, keepdims=True)
    p = jnp.exp(scores - m)
    den = jnp.sum(p, axis=-1, keepdims=True)
    o = jnp.einsum('bdhnqk,bdhnkc->bdhnqc', p, vw.astype(jnp.float32)) / den
    lse = (m + jnp.log(den))[..., 0]
    o = o.reshape(b, dilation, h, Lp, c)[:, :, :, :L].transpose(0, 3, 1, 2, 4).reshape(b, t, h, c)
    lse = lse.reshape(b, dilation, h, Lp)[..., :L].transpose(0, 3, 1, 2).reshape(b, t, h)
    return o, lse


def dilated_mixture(q, k, v):
    outs, lses = [], []
    for window, dilation in DILATED_PATTERNS:
        o, lse = dilated_branch(q, k, v, window, dilation)
        outs.append(o)
        lses.append(lse)
    w = jax.nn.softmax(jnp.stack(lses, axis=0), axis=0)
    return jnp.sum(w[..., None] * jnp.stack(outs, axis=0), axis=0).astype(q.dtype)


def pooling_mixer(xb, w_group, scale):
    b, t, _ = xb.shape
    xf = xb.astype(jnp.float32)
    cs = jnp.pad(jnp.cumsum(xf, axis=1), ((0, 0), (1, 0), (0, 0)))
    pos = jnp.arange(t)
    outs = []
    for g, p in enumerate(POOL_SIZES):
        lo_c, hi_c = g * POOL_GROUP, (g + 1) * POOL_GROUP
        csg = cs[..., lo_c:hi_c]
        lo = jnp.maximum(pos + 1 - p, 0)
        cnt = jnp.minimum(pos + 1, p).astype(jnp.float32)[None, :, None]
        pooled = (csg[:, 1:] - csg[:, lo]) / cnt - xf[..., lo_c:hi_c]
        outs.append(pooled @ w_group[g].astype(jnp.float32))
    y = jnp.concatenate(outs, axis=-1) * scale.astype(jnp.float32)
    return y.astype(xb.dtype)


def even_mixer(h, positions, w_in, q_gain, k_gain, w_group, scale, w_out):
    b, t, _ = h.shape
    proj = h @ w_in
    q, k, v, pb = jnp.split(proj, [A_WIDTH, 2 * A_WIDTH, 3 * A_WIDTH], axis=-1)
    q = apply_rope(rms_norm(q.reshape(b, t, A_HEADS, HEAD_DIM), q_gain), positions)
    k = apply_rope(rms_norm(k.reshape(b, t, A_HEADS, HEAD_DIM), k_gain), positions)
    v = v.reshape(b, t, A_HEADS, HEAD_DIM)
    o_a = dilated_mixture(q, k, v).reshape(b, t, A_WIDTH)
    o_b = pooling_mixer(pb, w_group, scale)
    return jnp.concatenate([o_a, o_b], axis=-1) @ w_out


def odd_mixer(h, w_in, v_gain, w_spatial, b_spatial, w_out):
    b, t, _ = h.shape
    z = jax.nn.gelu(h @ w_in)
    u, v = jnp.split(z, 2, axis=-1)
    v = rms_norm(v, v_gain)
    nc = t // GM_CHUNK
    vc = v.reshape(b, nc, GM_CHUNK, GM_GROUPS, GM_GROUP_DIM)
    causal = jnp.tril(jnp.ones((GM_CHUNK, GM_CHUNK), dtype=bool))
    ws = jnp.where(causal[None], w_spatial, jnp.zeros_like(w_spatial))
    sv = jnp.einsum('gst,bnthc->bnshc', ws, vc) + b_spatial.T[None, None, :, :, None]
    gated = u * sv.reshape(b, t, GM_WIDTH)
    return gated @ w_out


def grouped_moe(h, router_w, router_bias, w_gate, w_up, w_down):
    b, t, _ = h.shape
    scores = jax.nn.softmax((h @ router_w).astype(jnp.float32), axis=-1)
    biased = scores + router_bias.astype(jnp.float32)
    grp = biased.reshape(b, t, N_EXPERT_GROUPS, EXPERTS_PER_GROUP)
    grp_score = jnp.sum(lax.top_k(grp, 2)[0], axis=-1)
    best = jnp.argmax(grp_score, axis=-1)
    in_group = jnp.repeat(jax.nn.one_hot(best, N_EXPERT_GROUPS, dtype=jnp.float32), EXPERTS_PER_GROUP, axis=-1) > 0
    masked = jnp.where(in_group, biased, -jnp.inf)
    _, idx = lax.top_k(masked, TOP_K)
    gate = jnp.take_along_axis(scores, idx, axis=-1)
    gate = gate / jnp.sum(gate, axis=-1, keepdims=True)
    combine = jnp.einsum('btk,btke->bte', gate, jax.nn.one_hot(idx, N_EXPERTS, dtype=jnp.float32))
    hg = jnp.einsum('btd,edf->btef', h, w_gate)
    hu = jnp.einsum('btd,edf->btef', h, w_up)
    act = jax.nn.silu(hg) * hu * combine.astype(h.dtype)[..., None]
    return jnp.einsum('btef,efd->btd', act, w_down)


def setup_inputs(seed: int = 0) -> dict:
    key = jax.random.key(seed)
    ks = jax.random.split(key, 20)
    n_even = (DEPTH + 1) // 2
    n_odd = DEPTH // 2
    out_scale = (2.0 * DEPTH) ** -0.5

    def normal(k, shape, scale):
        return jax.random.normal(k, shape, jnp.float32) * scale

    return {
        "x": normal(ks[0], (BATCH, SEQ, D_MODEL), 1.0),
        "positions": jnp.broadcast_to(jnp.arange(SEQ, dtype=jnp.int32), (BATCH, SEQ)),
        "norm_mix": 1.0 + normal(ks[1], (DEPTH, D_MODEL), 0.02),
        "norm_ffn": 1.0 + normal(ks[2], (DEPTH, D_MODEL), 0.02),
        "a_w_in": normal(ks[3], (n_even, D_MODEL, EVEN_IN), D_MODEL ** -0.5),
        "a_q_norm": 1.0 + normal(ks[4], (n_even, HEAD_DIM), 0.02),
        "a_k_norm": 1.0 + normal(ks[5], (n_even, HEAD_DIM), 0.02),
        "b_w_group": normal(ks[6], (n_even, len(POOL_SIZES), POOL_GROUP, POOL_GROUP), POOL_GROUP ** -0.5),
        "b_scale": 1.0 + normal(ks[7], (n_even, B_WIDTH), 0.1),
        "ab_w_out": normal(ks[8], (n_even, EVEN_CAT, D_MODEL), EVEN_CAT ** -0.5 * out_scale),
        "c_w_in": normal(ks[9], (n_odd, D_MODEL, 2 * GM_WIDTH), D_MODEL ** -0.5),
        "c_v_norm": 1.0 + normal(ks[10], (n_odd, GM_WIDTH), 0.02),
        "c_w_spatial": normal(ks[11], (n_odd, GM_GROUPS, GM_CHUNK, GM_CHUNK), GM_CHUNK ** -0.5),
        "c_b_spatial": 1.0 + normal(ks[12], (n_odd, GM_GROUPS, GM_CHUNK), 0.02),
        "c_w_out": normal(ks[13], (n_odd, GM_WIDTH, D_MODEL), GM_WIDTH ** -0.5 * out_scale),
        "router_w": normal(ks[14], (D_MODEL, N_EXPERTS), D_MODEL ** -0.5),
        "router_bias": normal(ks[15], (N_EXPERTS,), 0.01),
        "expert_w_gate": normal(ks[16], (DEPTH, N_EXPERTS, D_MODEL, D_EXPERT), D_MODEL ** -0.5),
        "expert_w_up": normal(ks[17], (DEPTH, N_EXPERTS, D_MODEL, D_EXPERT), D_MODEL ** -0.5),
        "expert_w_down": normal(ks[18], (DEPTH, N_EXPERTS, D_EXPERT, D_MODEL), D_EXPERT ** -0.5 * out_scale),
    }


def reference(x, positions, norm_mix, norm_ffn, a_w_in, a_q_norm, a_k_norm, b_w_group, b_scale,
              ab_w_out, c_w_in, c_v_norm, c_w_spatial, c_b_spatial, c_w_out, router_w, router_bias,
              expert_w_gate, expert_w_up, expert_w_down):
    for layer in range(DEPTH):
        i = layer // 2
        h = rms_norm(x, norm_mix[layer])
        if layer % 2 == 0:
            mix = even_mixer(h, positions, a_w_in[i], a_q_norm[i], a_k_norm[i], b_w_group[i], b_scale[i], ab_w_out[i])
        else:
            mix = odd_mixer(h, c_w_in[i], c_v_norm[i], c_w_spatial[i], c_b_spatial[i], c_w_out[i])
        x = x + mix.astype(x.dtype)
        h = rms_norm(x, norm_ffn[layer])
        x = x + grouped_moe(h, router_w, router_bias, expert_w_gate[layer], expert_w_up[layer], expert_w_down[layer]).astype(x.dtype)
    return x
```

```python
import functools

import jax
import jax.numpy as jnp
import numpy as np
from jax import lax
from jax.experimental import pallas as pl
from jax.experimental.pallas import tpu as pltpu

F32 = jnp.float32
BF16 = jnp.bfloat16

EPS = 1e-6
HEAD_DIM = 128
ROPE_DIM = HEAD_DIM // 4
ROPE_HALF = ROPE_DIM // 2
ROPE_THETA = 500000.0
POOL_SIZES = (2, 4, 8, 16)
N_STREAM = 16
ATTN_BLOCK = 128
DILATIONS = (16, 4, 1)
GM_GROUPS = 8
GM_CHUNK = 128
N_EXPERTS = 16
N_EXPERT_GROUPS = 4
EXPERTS_PER_GROUP = 4
PAIRS = ((0, 1), (0, 2), (0, 3), (1, 2), (1, 3), (2, 3))
N_BUCKETS = N_EXPERT_GROUPS * len(PAIRS)
MOE_TILE = 256
V7X_VMEM_LIMIT = 56 * 1024 * 1024


def _cparams(*sem):
    return pltpu.CompilerParams(dimension_semantics=sem, vmem_limit_bytes=V7X_VMEM_LIMIT)


def _rms(xf, gain_row):
    ms = jnp.mean(xf * xf, axis=-1, keepdims=True)
    return xf * lax.rsqrt(ms + EPS) * gain_row


def _trig_kernel(pos_ref, freq_ref, cos_ref, sin_ref):
    ang = pos_ref[...].astype(F32) * freq_ref[...]
    lane = lax.broadcasted_iota(jnp.int32, ang.shape, 1)
    c = jnp.cos(ang)
    s = jnp.sin(ang)
    cos_ref[...] = jnp.where(lane < ROPE_DIM, c, 1.0)
    sin_ref[...] = jnp.where(lane < ROPE_HALF, -s, jnp.where(lane < ROPE_DIM, s, 0.0))


def _trig_tables(pos_rows):
    t = pos_rows.shape[0]
    inv_freq = ROPE_THETA ** (-jnp.arange(ROPE_HALF, dtype=F32) / ROPE_HALF)
    freq = jnp.tile(inv_freq, HEAD_DIM // ROPE_HALF)[None, :]
    pos_b = jnp.broadcast_to(pos_rows[:, None], (t, HEAD_DIM))
    bm = 1024
    return pl.pallas_call(
        _trig_kernel,
        grid=(t // bm,),
        in_specs=[pl.BlockSpec((bm, HEAD_DIM), lambda i: (i, 0)),
                  pl.BlockSpec((1, HEAD_DIM), lambda i: (0, 0))],
        out_specs=[pl.BlockSpec((bm, HEAD_DIM), lambda i: (i, 0))] * 2,
        out_shape=[jax.ShapeDtypeStruct((t, HEAD_DIM), F32)] * 2,
        compiler_params=_cparams("arbitrary"),
        name="rope_tables",
    )(pos_b, freq)


def _even_in_kernel(x_ref, gain_ref, w_ref, qkg_ref, cos_ref, sin_ref, o_ref, h_scr, *, n_qk_blocks):
    j = pl.program_id(1)

    bs = o_ref.shape[1]
    bm = N_STREAM * bs

    @pl.when(j == 0)
    def _():
        i = lax.broadcasted_iota(jnp.int32, (bm, bm), 0)
        c = lax.broadcasted_iota(jnp.int32, (bm, bm), 1)
        perm = jnp.where(c == (i % bs) * N_STREAM + i // bs, 1.0, 0.0).astype(BF16)
        h = _rms(x_ref[...], gain_ref[...]).astype(BF16)
        h_scr[...] = jnp.dot(perm, h, preferred_element_type=F32).astype(BF16)

    acc = jnp.dot(h_scr[...], w_ref[...], preferred_element_type=F32)

    @pl.when(j < n_qk_blocks)
    def _():
        gain = jnp.where(j < n_qk_blocks // 2, qkg_ref[0:1, :], qkg_ref[1:2, :])
        cos = cos_ref[...].reshape(bm, HEAD_DIM)
        sin = sin_ref[...].reshape(bm, HEAD_DIM)
        lane = lax.broadcasted_iota(jnp.int32, cos.shape, 1)
        for hh in range(acc.shape[1] // HEAD_DIM):
            y = _rms(acc[:, hh * HEAD_DIM:(hh + 1) * HEAD_DIM], gain)
            swapped = jnp.where(lane < ROPE_HALF,
                                pltpu.roll(y, HEAD_DIM - ROPE_HALF, 1),
                                pltpu.roll(y, ROPE_HALF, 1))
            o_ref[:, :, hh * HEAD_DIM:(hh + 1) * HEAD_DIM] = (y * cos + swapped * sin).reshape(N_STREAM, bs, HEAD_DIM)

    @pl.when(j >= n_qk_blocks)
    def _():
        o_ref[...] = acc.reshape(o_ref.shape)


def _even_in_proj(x, gain, w_bf16, qk_gain, cos3, sin3, a_width):
    t, d = x.shape
    n = w_bf16.shape[1]
    ns = t // N_STREAM
    bm, bn = 512, 512
    bs = bm // N_STREAM
    kern = functools.partial(_even_in_kernel, n_qk_blocks=2 * a_width // bn)
    return pl.pallas_call(
        kern,
        grid=(t // bm, n // bn),
        in_specs=[pl.BlockSpec((bm, d), lambda i, j: (i, 0)),
                  pl.BlockSpec((1, d), lambda i, j: (0, 0)),
                  pl.BlockSpec((d, bn), lambda i, j: (0, j)),
                  pl.BlockSpec((2, HEAD_DIM), lambda i, j: (0, 0)),
                  pl.BlockSpec((N_STREAM, bs, HEAD_DIM), lambda i, j: (0, i, 0)),
                  pl.BlockSpec((N_STREAM, bs, HEAD_DIM), lambda i, j: (0, i, 0))],
        out_specs=pl.BlockSpec((N_STREAM, bs, bn), lambda i, j: (0, i, j)),
        out_shape=jax.ShapeDtypeStruct((N_STREAM, ns, n), F32),
        scratch_shapes=[pltpu.VMEM((bm, d), BF16)],
        compiler_params=_cparams("arbitrary", "arbitrary"),
        name="even_in_proj",
    )(x, gain[None, :], w_bf16, qk_gain, cos3, sin3)


def _attn_kernel(q_ref, k_ref, v_ref, o_ref, o_scr, l_scr):
    sb = pl.program_id(1)
    scale = HEAD_DIM ** -0.5
    row = lax.broadcasted_iota(jnp.int32, (ATTN_BLOCK, 2 * ATTN_BLOCK), 0)
    col = lax.broadcasted_iota(jnp.int32, (ATTN_BLOCK, 2 * ATTN_BLOCK), 1)
    is_prev = col < ATTN_BLOCK
    colk = col & (ATTN_BLOCK - 1)

    for pat, d in enumerate(DILATIONS):
        c = N_STREAM // d
        cl = ATTN_BLOCK // c
        sh = cl.bit_length() - 1
        qpos = c * (row & (cl - 1)) + (row >> sh)
        kpos = c * (colk & (cl - 1)) + (colk >> sh)
        bias = jnp.where(is_prev,
                         jnp.where(qpos <= kpos, 0.0, -jnp.inf),
                         jnp.where(qpos >= kpos, 0.0, -jnp.inf)).astype(F32)

        def block(b, carry, d=d, c=c, cl=cl, pat=pat, bias=bias):
            r_d = b // c
            jj = b % c
            lo = pl.multiple_of(jj * cl, cl)
            u0 = sb * ATTN_BLOCK + lo
            up = pl.multiple_of(jnp.maximum(u0 - cl, 0), cl)
            u0 = pl.multiple_of(u0, cl)
            first = (sb * c + jj) == 0
            qs = [q_ref[r_d + d * a, pl.ds(lo, cl), :] for a in range(c)]
            ks = ([k_ref[r_d + d * a, pl.ds(up, cl), :] for a in range(c)]
                  + [k_ref[r_d + d * a, pl.ds(u0, cl), :] for a in range(c)])
            vs = ([v_ref[r_d + d * a, pl.ds(up, cl), :] for a in range(c)]
                  + [v_ref[r_d + d * a, pl.ds(u0, cl), :] for a in range(c)])
            q = jnp.concatenate(qs, axis=0).astype(BF16) if c > 1 else qs[0].astype(BF16)
            k = jnp.concatenate(ks, axis=0).astype(BF16)
            v = jnp.concatenate(vs, axis=0).astype(BF16)
            s = lax.dot_general(q, k, (((1,), (1,)), ((), ())), preferred_element_type=F32) * scale + bias
            s = jnp.where(is_prev & first, -jnp.inf, s)
            m = jnp.max(s, axis=-1, keepdims=True)
            p = jnp.exp(s - m)
            den = jnp.sum(p, axis=-1, keepdims=True)
            o = jnp.dot(p.astype(BF16), v, preferred_element_type=F32) / den
            lse = jnp.broadcast_to(m + jnp.log(den), (ATTN_BLOCK, HEAD_DIM))
            for a in range(c):
                o_scr[pat, r_d + d * a, pl.ds(lo, cl), :] = o[a * cl:(a + 1) * cl, :]
                l_scr[pat, r_d + d * a, pl.ds(lo, cl), :] = lse[a * cl:(a + 1) * cl, :]
            return carry

        lax.fori_loop(0, N_STREAM, block, 0, unroll=2)

    def mix(r, carry):
        l0, l1, l2 = l_scr[0, r], l_scr[1, r], l_scr[2, r]
        mx = jnp.maximum(jnp.maximum(l0, l1), l2)
        w0, w1, w2 = jnp.exp(l0 - mx), jnp.exp(l1 - mx), jnp.exp(l2 - mx)
        num = w0 * o_scr[0, r] + w1 * o_scr[1, r] + w2 * o_scr[2, r]
        o_ref[r] = (num / (w0 + w1 + w2)).astype(o_ref.dtype)
        return carry

    lax.fori_loop(0, N_STREAM, mix, 0)


def _dilated_attention(proj3, n_heads):
    _, ns, n = proj3.shape
    n_sb = ns // ATTN_BLOCK
    out3 = pl.pallas_call(
        _attn_kernel,
        grid=(n_heads, n_sb),
        in_specs=[pl.BlockSpec((N_STREAM, ATTN_BLOCK, HEAD_DIM), lambda h, sb: (0, sb, h)),
                  pl.BlockSpec((N_STREAM, ns, HEAD_DIM), lambda h, sb: (0, 0, n_heads + h)),
                  pl.BlockSpec((N_STREAM, ns, HEAD_DIM), lambda h, sb: (0, 0, 2 * n_heads + h))],
        out_specs=pl.BlockSpec((N_STREAM, ATTN_BLOCK, HEAD_DIM), lambda h, sb: (0, sb, h)),
        out_shape=jax.ShapeDtypeStruct((N_STREAM, ns, n_heads * HEAD_DIM), BF16),
        scratch_shapes=[pltpu.VMEM((3, N_STREAM, ATTN_BLOCK, HEAD_DIM), F32),
                        pltpu.VMEM((3, N_STREAM, ATTN_BLOCK, HEAD_DIM), F32)],
        compiler_params=_cparams("arbitrary", "arbitrary"),
        name="dilated_attention",
    )(proj3, proj3, proj3)
    return out3


def _pool_kernel(pb_ref, w_ref, sc_ref, o_ref, pre_scr):
    g = pl.program_id(0)
    ns, gw = pb_ref.shape[1], pb_ref.shape[2]
    first = lax.broadcasted_iota(jnp.int32, (ns, gw), 0) == 0

    for r in range(N_STREAM):
        pre_scr[r] = pb_ref[r] if r == 0 else pre_scr[r - 1] + pb_ref[r]

    for gi, p in enumerate(POOL_SIZES):
        @pl.when(g == gi)
        def _(p=p):
            for r in range(N_STREAM):
                win = pre_scr[r] - pre_scr[r - p] if r - p >= 0 else pre_scr[r]
                if r - p + 1 < 0:
                    wrap = pre_scr[N_STREAM - 1] - pre_scr[r - p + N_STREAM]
                    win = win + jnp.where(first, 0.0, pltpu.roll(wrap, 1, 0))
                cnt = jnp.where(first, float(min(r + 1, p)), float(p))
                pooled = win / cnt - pb_ref[r]
                y = jnp.dot(pooled.astype(BF16), w_ref[...], preferred_element_type=F32) * sc_ref[...]
                o_ref[r] = y.astype(o_ref.dtype)


def _pooling_mixer(proj3, w_group_bf16, scale):
    _, ns, n = proj3.shape
    n_groups, gw, _ = w_group_bf16.shape
    pb_blk0 = (n - n_groups * gw) // gw
    return pl.pallas_call(
        _pool_kernel,
        grid=(n_groups,),
        in_specs=[pl.BlockSpec((N_STREAM, ns, gw), lambda g: (0, 0, pb_blk0 + g)),
                  pl.BlockSpec((None, gw, gw), lambda g: (g, 0, 0)),
                  pl.BlockSpec((1, gw), lambda g: (0, g))],
        out_specs=pl.BlockSpec((N_STREAM, ns, gw), lambda g: (0, 0, g)),
        out_shape=jax.ShapeDtypeStruct((N_STREAM, ns, n_groups * gw), BF16),
        scratch_shapes=[pltpu.VMEM((N_STREAM, ns, gw), F32)],
        compiler_params=_cparams("arbitrary"),
        name="pooling_mixer",
    )(proj3, w_group_bf16, scale[None, :])


def _even_out_kernel(a_ref, b_ref, wa_ref, wb_ref, x_ref, o_ref, a_scr, b_scr):
    bs = a_ref.shape[1]
    bm = N_STREAM * bs

    @pl.when(pl.program_id(1) == 0)
    def _():
        n = lax.broadcasted_iota(jnp.int32, (bm, bm), 0)
        c = lax.broadcasted_iota(jnp.int32, (bm, bm), 1)
        perm = jnp.where(c == (n % N_STREAM) * bs + n // N_STREAM, 1.0, 0.0).astype(BF16)
        a = a_ref[...].reshape(bm, a_ref.shape[2])
        b = b_ref[...].reshape(bm, b_ref.shape[2])
        a_scr[...] = jnp.dot(perm, a, preferred_element_type=F32).astype(BF16)
        b_scr[...] = jnp.dot(perm, b, preferred_element_type=F32).astype(BF16)

    mix = (jnp.dot(a_scr[...], wa_ref[...], preferred_element_type=F32)
           + jnp.dot(b_scr[...], wb_ref[...], preferred_element_type=F32))
    o_ref[...] = x_ref[...] + mix


def _even_out_proj(oa3, ob3, w_bf16, x):
    t, d = x.shape
    ka, kb = oa3.shape[2], ob3.shape[2]
    assert ka % kb == 0
    bm, bn = 512, 1024
    bs = bm // N_STREAM
    return pl.pallas_call(
        _even_out_kernel,
        grid=(t // bm, d // bn),
        in_specs=[pl.BlockSpec((N_STREAM, bs, ka), lambda i, j: (0, i, 0)),
                  pl.BlockSpec((N_STREAM, bs, kb), lambda i, j: (0, i, 0)),
                  pl.BlockSpec((ka, bn), lambda i, j: (0, j)),
                  pl.BlockSpec((kb, bn), lambda i, j: (ka // kb, j)),
                  pl.BlockSpec((bm, bn), lambda i, j: (i, j))],
        out_specs=pl.BlockSpec((bm, bn), lambda i, j: (i, j)),
        out_shape=jax.ShapeDtypeStruct((t, d), F32),
        scratch_shapes=[pltpu.VMEM((bm, ka), BF16), pltpu.VMEM((bm, kb), BF16)],
        compiler_params=_cparams("arbitrary", "arbitrary"),
        name="even_out_proj",
    )(oa3, ob3, w_bf16, w_bf16, x)


def _odd_in_kernel(x_ref, gain_ref, w_ref, o_ref, h_scr):
    @pl.when(pl.program_id(1) == 0)
    def _():
        h_scr[...] = _rms(x_ref[...], gain_ref[...]).astype(BF16)

    acc = jnp.dot(h_scr[...], w_ref[...], preferred_element_type=F32)
    o_ref[...] = jax.nn.gelu(acc).astype(o_ref.dtype)


def _odd_in_proj(x, gain, w_bf16):
    t, d = x.shape
    n = w_bf16.shape[1]
    bm, bn = 512, 1024
    return pl.pallas_call(
        _odd_in_kernel,
        grid=(t // bm, n // bn),
        in_specs=[pl.BlockSpec((bm, d), lambda i, j: (i, 0)),
                  pl.BlockSpec((1, d), lambda i, j: (0, 0)),
                  pl.BlockSpec((d, bn), lambda i, j: (0, j))],
        out_specs=pl.BlockSpec((bm, bn), lambda i, j: (i, j)),
        out_shape=jax.ShapeDtypeStruct((t, n), BF16),
        scratch_shapes=[pltpu.VMEM((bm, d), BF16)],
        compiler_params=_cparams("arbitrary", "arbitrary"),
        name="odd_in_proj",
    )(x, gain[None, :], w_bf16)


def _odd_out_kernel(u_ref, v_ref, vg_ref, ws_ref, bcol_ref, w_ref, x_ref, o_ref, vn_scr, g_scr):
    @pl.when(pl.program_id(1) == 0)
    def _():
        vn_scr[...] = _rms(v_ref[...].astype(F32), vg_ref[...]).astype(BF16)
        bm, width = vn_scr.shape
        gd = width // GM_GROUPS
        causal = (lax.broadcasted_iota(jnp.int32, (GM_CHUNK, GM_CHUNK), 0)
                  >= lax.broadcasted_iota(jnp.int32, (GM_CHUNK, GM_CHUNK), 1))
        wsum = jnp.where(causal, ws_ref[0], 0.0)
        for g in range(1, GM_GROUPS):
            wsum = wsum + jnp.where(causal, ws_ref[g], 0.0)
        wsum = wsum.astype(BF16)
        for g in range(GM_GROUPS):
            bcol = bcol_ref[:, g:g + 1]
            for cc in range(bm // GM_CHUNK):
                rows = slice(cc * GM_CHUNK, (cc + 1) * GM_CHUNK)
                cols = slice(g * gd, (g + 1) * gd)
                sv = jnp.dot(wsum, vn_scr[rows, cols], preferred_element_type=F32) + bcol
                g_scr[rows, cols] = (u_ref[rows, cols].astype(F32) * sv).astype(BF16)

    o_ref[...] = x_ref[...] + jnp.dot(g_scr[...], w_ref[...], preferred_element_type=F32)


def _odd_out_proj(z, v_gain, w_spatial, b_spatial, w_bf16, x):
    t, d = x.shape
    width = z.shape[1] // 2
    bm, bn = 512, 512
    return pl.pallas_call(
        _odd_out_kernel,
        grid=(t // bm, d // bn),
        in_specs=[pl.BlockSpec((bm, width), lambda i, j: (i, 0)),
                  pl.BlockSpec((bm, width), lambda i, j: (i, 1)),
                  pl.BlockSpec((1, width), lambda i, j: (0, 0)),
                  pl.BlockSpec((GM_GROUPS, GM_CHUNK, GM_CHUNK), lambda i, j: (0, 0, 0)),
                  pl.BlockSpec((GM_CHUNK, GM_GROUPS), lambda i, j: (0, 0)),
                  pl.BlockSpec((width, bn), lambda i, j: (0, j)),
                  pl.BlockSpec((bm, bn), lambda i, j: (i, j))],
        out_specs=pl.BlockSpec((bm, bn), lambda i, j: (i, j)),
        out_shape=jax.ShapeDtypeStruct((t, d), F32),
        scratch_shapes=[pltpu.VMEM((bm, width), BF16), pltpu.VMEM((bm, width), BF16)],
        compiler_params=_cparams("arbitrary", "arbitrary"),
        name="odd_out_proj",
    )(z, z, v_gain[None, :], w_spatial, b_spatial.T, w_bf16, x)


def _router_kernel(x_ref, gain_ref, w_ref, bias_ref, o_ref):
    h = _rms(x_ref[...], gain_ref[...])
    logits = jnp.dot(h, w_ref[...], preferred_element_type=F32, precision=lax.Precision.HIGHEST)
    lt = logits.T[0:N_EXPERTS, :]
    e = jnp.exp(lt - jnp.max(lt, axis=0, keepdims=True))
    scores = e / jnp.sum(e, axis=0, keepdims=True)
    biased = scores + bias_ref[...]
    sc = [scores[i:i + 1, :] for i in range(N_EXPERTS)]
    bi = [biased[i:i + 1, :] for i in range(N_EXPERTS)]

    def top2_sum(a, b, c, d):
        return jnp.maximum(jnp.maximum(jnp.maximum(a + b, a + c), jnp.maximum(a + d, b + c)),
                           jnp.maximum(b + d, c + d))

    grp = [top2_sum(*bi[EXPERTS_PER_GROUP * g:EXPERTS_PER_GROUP * (g + 1)]) for g in range(N_EXPERT_GROUPS)]
    best = jnp.zeros_like(grp[0], dtype=jnp.int32)
    best_score = grp[0]
    for g in range(1, N_EXPERT_GROUPS):
        upd = grp[g] > best_score
        best = jnp.where(upd, g, best)
        best_score = jnp.where(upd, grp[g], best_score)

    def pick(vals, k):
        out = vals[k]
        for g in range(1, N_EXPERT_GROUPS):
            out = jnp.where(best == g, vals[EXPERTS_PER_GROUP * g + k], out)
        return out

    vb = [pick(bi, k) for k in range(EXPERTS_PER_GROUP)]
    vs = [pick(sc, k) for k in range(EXPERTS_PER_GROUP)]
    i1 = jnp.zeros_like(best)
    m1 = vb[0]
    for k in range(1, EXPERTS_PER_GROUP):
        upd = vb[k] > m1
        i1 = jnp.where(upd, k, i1)
        m1 = jnp.where(upd, vb[k], m1)
    i2 = jnp.zeros_like(best)
    m2 = jnp.full_like(m1, -jnp.inf)
    for k in range(EXPERTS_PER_GROUP):
        upd = (i1 != k) & (vb[k] > m2)
        i2 = jnp.where(upd, k, i2)
        m2 = jnp.where(upd, vb[k], m2)

    def take(idx):
        out = vs[0]
        for k in range(1, EXPERTS_PER_GROUP):
            out = jnp.where(idx == k, vs[k], out)
        return out

    g1, g2 = take(i1), take(i2)
    tot = g1 + g2
    g1, g2 = g1 / tot, g2 / tot
    swap = i2 < i1
    lo = jnp.where(swap, i2, i1)
    hi = jnp.where(swap, i1, i2)
    o_ref[0:1, :] = best.astype(F32)
    o_ref[1:2, :] = lo.astype(F32)
    o_ref[2:3, :] = hi.astype(F32)
    o_ref[3:4, :] = jnp.where(swap, g2, g1)
    o_ref[4:5, :] = jnp.where(swap, g1, g2)
    o_ref[5:8, :] = jnp.zeros((3, best.shape[1]), F32)


def _router(x, gain, router_w_pad, router_bias):
    t, d = x.shape
    bm = 512
    return pl.pallas_call(
        _router_kernel,
        grid=(t // bm,),
        in_specs=[pl.BlockSpec((bm, d), lambda i: (i, 0)),
                  pl.BlockSpec((1, d), lambda i: (0, 0)),
                  pl.BlockSpec((d, HEAD_DIM), lambda i: (0, 0)),
                  pl.BlockSpec((N_EXPERTS, 1), lambda i: (0, 0))],
        out_specs=pl.BlockSpec((8, bm), lambda i: (0, i)),
        out_shape=jax.ShapeDtypeStruct((8, t), F32),
        compiler_params=_cparams("arbitrary"),
        name="moe_router",
    )(x, gain[None, :], router_w_pad, router_bias[:, None])


def _moe_plan(route, n_tiles):
    t = route.shape[1]
    best = route[0].astype(jnp.int32)
    lo = route[1].astype(jnp.int32)
    hi = route[2].astype(jnp.int32)
    pair_index = jnp.asarray(np.array([[0, 0, 1, 2], [0, 0, 3, 4], [0, 0, 0, 5], [0, 0, 0, 0]], np.int32))
    bucket = best * len(PAIRS) + pair_index[lo, hi]
    order = jnp.argsort(bucket).astype(jnp.int32)
    counts = jnp.sum(bucket[:, None] == jnp.arange(N_BUCKETS, dtype=jnp.int32)[None, :], axis=0).astype(jnp.int32)
    off = jnp.cumsum(counts) - counts
    tiles = (counts + MOE_TILE - 1) // MOE_TILE
    tile_end = jnp.cumsum(tiles)
    tile_start = tile_end - tiles
    total = tile_end[-1]
    tile_ids = jnp.arange(n_tiles, dtype=jnp.int32)
    tile_valid = (tile_ids < total).astype(jnp.int32)
    tile_bucket = jnp.searchsorted(tile_end, jnp.minimum(tile_ids, total - 1), side="right").astype(jnp.int32)
    rows = jnp.arange(n_tiles * MOE_TILE, dtype=jnp.int32)
    row_tile = rows // MOE_TILE
    row_bucket = tile_bucket[row_tile]
    rank = rows - tile_start[row_bucket] * MOE_TILE
    row_valid = (rank < counts[row_bucket]) & (tile_valid[row_tile] > 0)
    src = order[jnp.clip(off[row_bucket] + rank, 0, t - 1)]
    token_of = jnp.where(row_valid, src, -1).astype(jnp.int32)
    gates = jnp.where(row_valid[:, None], jnp.stack([route[3][src], route[4][src]], axis=1), 0.0)
    pairs = jnp.asarray(np.array(PAIRS, np.int32))
    tile_group = tile_bucket // len(PAIRS)
    tile_pair = pairs[tile_bucket % len(PAIRS)]
    odd = (tile_ids & 1)[:, None]
    slot = jnp.arange(2, dtype=jnp.int32)[None, :] ^ odd
    step_expert = (tile_group[:, None] * EXPERTS_PER_GROUP
                   + jnp.take_along_axis(tile_pair, slot, axis=1)).reshape(-1).astype(jnp.int32)
    step_expert = jnp.where(jnp.repeat(tile_valid, 2) > 0, step_expert, step_expert[2 * total - 1])
    return step_expert, tile_valid, token_of, gates


def _moe_kernel(se_ref, tv_ref, tok_ref, x_hbm, gates_ref, gain_ref, wg_ref, wu_ref, wd_ref,
                out_hbm, xbuf, hbuf, ybuf, sem_in, sem_out):
    del se_ref
    i = pl.program_id(0)
    s = pl.program_id(1)
    valid = tv_ref[i] > 0
    base = i * MOE_TILE

    def row_in(r, tok):
        return pltpu.make_async_copy(x_hbm.at[pl.ds(tok, 1)], xbuf.at[pl.ds(r, 1)], sem_in)

    def row_out(r, tok):
        return pltpu.make_async_copy(xbuf.at[pl.ds(r, 1)], out_hbm.at[pl.ds(tok, 1)], sem_out)

    @pl.when(valid & (s == 0))
    def _():
        def start(r, carry):
            row_in(r, jnp.maximum(tok_ref[base + r], 0)).start()
            return carry

        def wait(r, carry):
            row_in(r, 0).wait()
            return carry

        lax.fori_loop(0, MOE_TILE, start, 0)
        lax.fori_loop(0, MOE_TILE, wait, 0)
        hbuf[...] = _rms(xbuf[...], gain_ref[...]).astype(BF16)

    @pl.when(valid)
    def _():
        which = s ^ (i & 1)
        gate = jnp.where(which == 0, gates_ref[:, 0:1], gates_ref[:, 1:2])
        h = hbuf[...]
        hg = jnp.dot(h, wg_ref[...], preferred_element_type=F32)
        hu = jnp.dot(h, wu_ref[...], preferred_element_type=F32)
        act = (jax.nn.silu(hg) * hu * gate).astype(BF16)
        y = jnp.dot(act, wd_ref[...], preferred_element_type=F32)

        @pl.when(s == 0)
        def _():
            ybuf[...] = y

        @pl.when(s == 1)
        def _():
            xbuf[...] = xbuf[...] + (ybuf[...] + y)

            def start(r, carry):
                tok = tok_ref[base + r]

                @pl.when(tok >= 0)
                def _():
                    row_out(r, tok).start()
                return carry

            def wait(r, carry):
                @pl.when(tok_ref[base + r] >= 0)
                def _():
                    row_out(r, 0).wait()
                return carry

            lax.fori_loop(0, MOE_TILE, start, 0)
            lax.fori_loop(0, MOE_TILE, wait, 0)


def _grouped_moe(x, gain, router_w_pad, router_bias, wg_bf16, wu_bf16, wd_bf16):
    t, d = x.shape
    f = wg_bf16.shape[2]
    n_tiles = t // MOE_TILE + N_BUCKETS
    route = _router(x, gain, router_w_pad, router_bias)
    step_expert, tile_valid, token_of, gates = _moe_plan(route, n_tiles)
    grid_spec = pltpu.PrefetchScalarGridSpec(
        num_scalar_prefetch=3,
        grid=(n_tiles, 2),
        in_specs=[pl.BlockSpec(memory_space=pl.ANY),
                  pl.BlockSpec((MOE_TILE, 2), lambda i, s, se, tv, tok: (i, 0)),
                  pl.BlockSpec((1, d), lambda i, s, se, tv, tok: (0, 0)),
                  pl.BlockSpec((None, d, f), lambda i, s, se, tv, tok: (se[2 * i + s], 0, 0)),
                  pl.BlockSpec((None, d, f), lambda i, s, se, tv, tok: (se[2 * i + s], 0, 0)),
                  pl.BlockSpec((None, f, d), lambda i, s, se, tv, tok: (se[2 * i + s], 0, 0))],
        out_specs=pl.BlockSpec(memory_space=pl.ANY),
        scratch_shapes=[pltpu.VMEM((MOE_TILE, d), F32),
                        pltpu.VMEM((MOE_TILE, d), BF16),
                        pltpu.VMEM((MOE_TILE, d), F32),
                        pltpu.SemaphoreType.DMA(()),
                        pltpu.SemaphoreType.DMA(())],
    )
    return pl.pallas_call(
        _moe_kernel,
        grid_spec=grid_spec,
        out_shape=jax.ShapeDtypeStruct((t, d), F32),
        compiler_params=_cparams("arbitrary", "arbitrary"),
        name="moe_experts",
    )(step_expert, tile_valid, token_of, x, gates, gain[None, :], wg_bf16, wu_bf16, wd_bf16)


def kernel(x, positions, norm_mix, norm_ffn, a_w_in, a_q_norm, a_k_norm, b_w_group, b_scale, ab_w_out,
           c_w_in, c_v_norm, c_w_spatial, c_b_spatial, c_w_out, router_w, router_bias,
           expert_w_gate, expert_w_up, expert_w_down):
    batch, t, d = x.shape
    assert batch == 1 and t % (N_STREAM * ATTN_BLOCK) == 0
    depth = norm_mix.shape[0]
    a_width = (a_w_in.shape[2] - b_w_group.shape[1] * b_w_group.shape[2]) // 3
    n_heads = a_width // HEAD_DIM
    ns = t // N_STREAM

    xs = x[0]
    pos_rows = positions[0].reshape(ns, N_STREAM).T.reshape(t)
    cos, sin = (tab.reshape(N_STREAM, ns, HEAD_DIM) for tab in _trig_tables(pos_rows))
    router_w_pad = jnp.pad(router_w, ((0, 0), (0, HEAD_DIM - N_EXPERTS)))

    for layer in range(depth):
        i = layer // 2
        if layer % 2 == 0:
            proj = _even_in_proj(xs, norm_mix[layer], a_w_in[i].astype(BF16),
                                 jnp.stack([a_q_norm[i], a_k_norm[i]]), cos, sin, a_width)
            oa3 = _dilated_attention(proj, n_heads)
            ob3 = _pooling_mixer(proj, b_w_group[i].astype(BF16), b_scale[i])
            xs = _even_out_proj(oa3, ob3, ab_w_out[i].astype(BF16), xs)
        else:
            z = _odd_in_proj(xs, norm_mix[layer], c_w_in[i].astype(BF16))
            xs = _odd_out_proj(z, c_v_norm[i], c_w_spatial[i], c_b_spatial[i], c_w_out[i].astype(BF16), xs)
        xs = _grouped_moe(xs, norm_ffn[layer], router_w_pad, router_bias,
                          expert_w_gate[layer].astype(BF16), expert_w_up[layer].astype(BF16),
                          expert_w_down[layer].astype(BF16))
    return xs[None]
```

```python
import functools

import jax
import jax.numpy as jnp
from jax import lax
from jax.experimental import pallas as pl
from jax.experimental.pallas import tpu as pltpu

F32 = jnp.float32
BF16 = jnp.bfloat16

EPS = 1e-6
HEAD_DIM = 128
ROPE_DIM = HEAD_DIM // 4
ROPE_HALF = ROPE_DIM // 2
ROPE_THETA = 500000.0
POOL_SIZES = (2, 4, 8, 16)
N_STREAM = 16
ATTN_BLOCK = 128
DILATIONS = (16, 4, 1)
GM_GROUPS = 8
GM_CHUNK = 128
N_EXPERTS = 16
N_EXPERT_GROUPS = 4
EXPERTS_PER_GROUP = 4
N_PAIRS = 6
N_BUCKETS = N_EXPERT_GROUPS * N_PAIRS
MOE_TILE = 256
V7X_VMEM_LIMIT = 56 * 1024 * 1024


def _cparams(*sem):
    return pltpu.CompilerParams(dimension_semantics=sem, vmem_limit_bytes=V7X_VMEM_LIMIT)


def _rms(xf, gain_row):
    ms = jnp.mean(xf * xf, axis=-1, keepdims=True)
    return xf * lax.rsqrt(ms + EPS) * gain_row


def _rows_2d(x3_ref, x2_scr):
    x2_scr[...] = x3_ref[...].reshape(x2_scr.shape)
    return x2_scr[...]


def _trig_kernel(pos_ref, freq_ref, cos_ref, sin_ref):
    ang = pos_ref[...].astype(F32) * freq_ref[...]
    lane = lax.broadcasted_iota(jnp.int32, ang.shape, 1)
    c = jnp.cos(ang)
    s = jnp.sin(ang)
    cos_ref[...] = jnp.where(lane < ROPE_DIM, c, 1.0)
    sin_ref[...] = jnp.where(lane < ROPE_HALF, -s, jnp.where(lane < ROPE_DIM, s, 0.0))


def _trig_tables(pos_rows):
    t = pos_rows.shape[0]
    inv_freq = ROPE_THETA ** (-jnp.arange(ROPE_HALF, dtype=F32) / ROPE_HALF)
    freq = jnp.tile(inv_freq, HEAD_DIM // ROPE_HALF)[None, :]
    pos_b = jnp.broadcast_to(pos_rows[:, None], (t, HEAD_DIM))
    bm = 1024
    return pl.pallas_call(
        _trig_kernel,
        grid=(t // bm,),
        in_specs=[pl.BlockSpec((bm, HEAD_DIM), lambda i: (i, 0)),
                  pl.BlockSpec((1, HEAD_DIM), lambda i: (0, 0))],
        out_specs=[pl.BlockSpec((bm, HEAD_DIM), lambda i: (i, 0))] * 2,
        out_shape=[jax.ShapeDtypeStruct((t, HEAD_DIM), F32)] * 2,
        compiler_params=_cparams("arbitrary"),
        name="rope_tables",
    )(pos_b, freq)


def _even_in_kernel(x_ref, gain_ref, w_ref, qkg_ref, cos_ref, sin_ref, o_ref, x2_scr, h_scr, *, n_qk_blocks):
    j = pl.program_id(1)

    bs = o_ref.shape[1]
    bm = N_STREAM * bs

    @pl.when(j == 0)
    def _():
        i = lax.broadcasted_iota(jnp.int32, (bm, bm), 0)
        c = lax.broadcasted_iota(jnp.int32, (bm, bm), 1)
        perm = jnp.where(c == (i % bs) * N_STREAM + i // bs, 1.0, 0.0).astype(BF16)
        h = _rms(_rows_2d(x_ref, x2_scr), gain_ref[...]).astype(BF16)
        h_scr[...] = jnp.dot(perm, h, preferred_element_type=F32).astype(BF16)

    acc = jnp.dot(h_scr[...], w_ref[...], preferred_element_type=F32)

    @pl.when(j < n_qk_blocks)
    def _():
        gain = jnp.where(j < n_qk_blocks // 2, qkg_ref[0:1, :], qkg_ref[1:2, :])
        cos = cos_ref[...].reshape(bm, HEAD_DIM)
        sin = sin_ref[...].reshape(bm, HEAD_DIM)
        lane = lax.broadcasted_iota(jnp.int32, cos.shape, 1)
        for hh in range(acc.shape[1] // HEAD_DIM):
            y = _rms(acc[:, hh * HEAD_DIM:(hh + 1) * HEAD_DIM], gain)
            swapped = jnp.where(lane < ROPE_HALF,
                                pltpu.roll(y, HEAD_DIM - ROPE_HALF, 1),
                                pltpu.roll(y, ROPE_HALF, 1))
            o_ref[:, :, hh * HEAD_DIM:(hh + 1) * HEAD_DIM] = (y * cos + swapped * sin).reshape(N_STREAM, bs, HEAD_DIM)

    @pl.when(j >= n_qk_blocks)
    def _():
        o_ref[...] = acc.reshape(o_ref.shape)


def _even_in_proj(x3, gain, w_bf16, li, qk_gain, cos3, sin3, a_width):
    t, _, d = x3.shape
    n = w_bf16.shape[2]
    ns = t // N_STREAM
    bm, bn = 512, 512
    bs = bm // N_STREAM
    kern = functools.partial(_even_in_kernel, n_qk_blocks=2 * a_width // bn)
    return pl.pallas_call(
        kern,
        grid=(t // bm, n // bn),
        in_specs=[pl.BlockSpec((bm, 1, d), lambda i, j: (i, 0, 0)),
                  pl.BlockSpec((1, d), lambda i, j: (0, 0)),
                  pl.BlockSpec((None, d, bn), lambda i, j: (li, 0, j)),
                  pl.BlockSpec((2, HEAD_DIM), lambda i, j: (0, 0)),
                  pl.BlockSpec((N_STREAM, bs, HEAD_DIM), lambda i, j: (0, i, 0)),
                  pl.BlockSpec((N_STREAM, bs, HEAD_DIM), lambda i, j: (0, i, 0))],
        out_specs=pl.BlockSpec((N_STREAM, bs, bn), lambda i, j: (0, i, j)),
        out_shape=jax.ShapeDtypeStruct((N_STREAM, ns, n), F32),
        scratch_shapes=[pltpu.VMEM((bm, d), F32), pltpu.VMEM((bm, d), BF16)],
        compiler_params=_cparams("arbitrary", "arbitrary"),
        name="even_in_proj",
    )(x3, gain[None, :], w_bf16, qk_gain, cos3, sin3)


def _attn_kernel(q_ref, k_ref, v_ref, o_ref, o_scr, l_scr):
    sb = pl.program_id(1)
    scale = HEAD_DIM ** -0.5
    base = pl.multiple_of(sb * ATTN_BLOCK, ATTN_BLOCK)
    prev_base = pl.multiple_of(jnp.maximum(base - ATTN_BLOCK, 0), ATTN_BLOCK)
    row = lax.broadcasted_iota(jnp.int32, (ATTN_BLOCK, 2 * ATTN_BLOCK), 0)
    col = lax.broadcasted_iota(jnp.int32, (ATTN_BLOCK, 2 * ATTN_BLOCK), 1)
    is_prev = col < ATTN_BLOCK
    colk = col & (ATTN_BLOCK - 1)

    for pat, d in enumerate(DILATIONS):
        c = N_STREAM // d
        cl = ATTN_BLOCK // c
        sh = cl.bit_length() - 1
        qpos = c * (row & (cl - 1)) + (row >> sh)
        kpos = c * (colk & (cl - 1)) + (colk >> sh)
        bias = jnp.where(is_prev,
                         jnp.where(qpos <= kpos, 0.0, -jnp.inf),
                         jnp.where(qpos >= kpos, 0.0, -jnp.inf)).astype(F32)
        bias_start = jnp.where(jnp.logical_and(is_prev, sb == 0), -jnp.inf, bias)

        for r_d in range(d):
            for jj in range(c):
                lo = jj * cl
                rows = [r_d + d * a for a in range(c)]
                if jj == 0:
                    prev = pl.ds(prev_base + (ATTN_BLOCK - cl), cl)
                else:
                    prev = pl.ds(base + (lo - cl), cl)
                cur = pl.ds(base + lo, cl)
                q = jnp.concatenate([q_ref[r, lo:lo + cl, :] for r in rows], axis=0).astype(BF16)
                k = jnp.concatenate([k_ref[r, prev, :] for r in rows] + [k_ref[r, cur, :] for r in rows],
                                    axis=0).astype(BF16)
                v = jnp.concatenate([v_ref[r, prev, :] for r in rows] + [v_ref[r, cur, :] for r in rows],
                                    axis=0).astype(BF16)
                s = lax.dot_general(q, k, (((1,), (1,)), ((), ())), preferred_element_type=F32) * scale
                s = s + (bias_start if jj == 0 else bias)
                m = jnp.max(s, axis=-1, keepdims=True)
                p = jnp.exp(s - m)
                den = jnp.sum(p, axis=-1, keepdims=True)
                o = jnp.dot(p.astype(BF16), v, preferred_element_type=F32) / den
                lse = jnp.broadcast_to(m + jnp.log(den), (ATTN_BLOCK, HEAD_DIM))
                for a, r in enumerate(rows):
                    o_scr[pat, r, lo:lo + cl, :] = o[a * cl:(a + 1) * cl, :]
                    l_scr[pat, r, lo:lo + cl, :] = lse[a * cl:(a + 1) * cl, :]

    def mix(r, carry):
        l0, l1, l2 = l_scr[0, r], l_scr[1, r], l_scr[2, r]
        mx = jnp.maximum(jnp.maximum(l0, l1), l2)
        w0, w1, w2 = jnp.exp(l0 - mx), jnp.exp(l1 - mx), jnp.exp(l2 - mx)
        num = w0 * o_scr[0, r] + w1 * o_scr[1, r] + w2 * o_scr[2, r]
        o_ref[r] = (num / (w0 + w1 + w2)).astype(o_ref.dtype)
        return carry

    lax.fori_loop(0, N_STREAM, mix, 0)


def _dilated_attention(proj3, n_heads):
    _, ns, n = proj3.shape
    n_sb = ns // ATTN_BLOCK
    out3 = pl.pallas_call(
        _attn_kernel,
        grid=(n_heads, n_sb),
        in_specs=[pl.BlockSpec((N_STREAM, ATTN_BLOCK, HEAD_DIM), lambda h, sb: (0, sb, h)),
                  pl.BlockSpec((N_STREAM, ns, HEAD_DIM), lambda h, sb: (0, 0, n_heads + h)),
                  pl.BlockSpec((N_STREAM, ns, HEAD_DIM), lambda h, sb: (0, 0, 2 * n_heads + h))],
        out_specs=pl.BlockSpec((N_STREAM, ATTN_BLOCK, HEAD_DIM), lambda h, sb: (0, sb, h)),
        out_shape=jax.ShapeDtypeStruct((N_STREAM, ns, n_heads * HEAD_DIM), BF16),
        scratch_shapes=[pltpu.VMEM((3, N_STREAM, ATTN_BLOCK, HEAD_DIM), F32),
                        pltpu.VMEM((3, N_STREAM, ATTN_BLOCK, HEAD_DIM), F32)],
        compiler_params=_cparams("arbitrary", "arbitrary"),
        name="dilated_attention",
    )(proj3, proj3, proj3)
    return out3


def _pool_kernel(pb_ref, w_ref, sc_ref, o_ref, pre_scr):
    g = pl.program_id(0)
    ns, gw = pb_ref.shape[1], pb_ref.shape[2]
    first = lax.broadcasted_iota(jnp.int32, (ns, gw), 0) == 0

    for r in range(N_STREAM):
        pre_scr[r] = pb_ref[r] if r == 0 else pre_scr[r - 1] + pb_ref[r]

    for gi, p in enumerate(POOL_SIZES):
        @pl.when(g == gi)
        def _(p=p):
            for r in range(N_STREAM):
                win = pre_scr[r] - pre_scr[r - p] if r - p >= 0 else pre_scr[r]
                if r - p + 1 < 0:
                    wrap = pre_scr[N_STREAM - 1] - pre_scr[r - p + N_STREAM]
                    win = win + jnp.where(first, 0.0, pltpu.roll(wrap, 1, 0))
                cnt = jnp.where(first, float(min(r + 1, p)), float(p))
                pooled = win / cnt - pb_ref[r]
                y = jnp.dot(pooled.astype(BF16), w_ref[...], preferred_element_type=F32) * sc_ref[...]
                o_ref[r] = y.astype(o_ref.dtype)


def _pooling_mixer(proj3, w_group_bf16, li, scale):
    _, ns, n = proj3.shape
    _, n_groups, gw, _ = w_group_bf16.shape
    pb_blk0 = (n - n_groups * gw) // gw
    return pl.pallas_call(
        _pool_kernel,
        grid=(n_groups,),
        in_specs=[pl.BlockSpec((N_STREAM, ns, gw), lambda g: (0, 0, pb_blk0 + g)),
                  pl.BlockSpec((None, None, gw, gw), lambda g: (li, g, 0, 0)),
                  pl.BlockSpec((1, gw), lambda g: (0, g))],
        out_specs=pl.BlockSpec((N_STREAM, ns, gw), lambda g: (0, 0, g)),
        out_shape=jax.ShapeDtypeStruct((N_STREAM, ns, n_groups * gw), BF16),
        scratch_shapes=[pltpu.VMEM((N_STREAM, ns, gw), F32)],
        compiler_params=_cparams("arbitrary"),
        name="pooling_mixer",
    )(proj3, w_group_bf16, scale[None, :])


def _even_out_kernel(a_ref, b_ref, wa_ref, wb_ref, x_ref, o_ref, a_scr, b_scr):
    bs = a_ref.shape[1]
    bm = N_STREAM * bs

    @pl.when(pl.program_id(1) == 0)
    def _():
        n = lax.broadcasted_iota(jnp.int32, (bm, bm), 0)
        c = lax.broadcasted_iota(jnp.int32, (bm, bm), 1)
        perm = jnp.where(c == (n % N_STREAM) * bs + n // N_STREAM, 1.0, 0.0).astype(BF16)
        a = a_ref[...].reshape(bm, a_ref.shape[2])
        b = b_ref[...].reshape(bm, b_ref.shape[2])
        a_scr[...] = jnp.dot(perm, a, preferred_element_type=F32).astype(BF16)
        b_scr[...] = jnp.dot(perm, b, preferred_element_type=F32).astype(BF16)

    mix = (jnp.dot(a_scr[...], wa_ref[...], preferred_element_type=F32)
           + jnp.dot(b_scr[...], wb_ref[...], preferred_element_type=F32))
    o_ref[...] = x_ref[...] + mix.reshape(o_ref.shape)


def _even_out_proj(oa3, ob3, w_bf16, li, x3):
    t, _, d = x3.shape
    ka, kb = oa3.shape[2], ob3.shape[2]
    assert ka % kb == 0
    bm, bn = 512, 1024
    bs = bm // N_STREAM
    return pl.pallas_call(
        _even_out_kernel,
        grid=(t // bm, d // bn),
        in_specs=[pl.BlockSpec((N_STREAM, bs, ka), lambda i, j: (0, i, 0)),
                  pl.BlockSpec((N_STREAM, bs, kb), lambda i, j: (0, i, 0)),
                  pl.BlockSpec((None, ka, bn), lambda i, j: (li, 0, j)),
                  pl.BlockSpec((None, kb, bn), lambda i, j: (li, ka // kb, j)),
                  pl.BlockSpec((bm, 1, bn), lambda i, j: (i, 0, j))],
        out_specs=pl.BlockSpec((bm, 1, bn), lambda i, j: (i, 0, j)),
        out_shape=jax.ShapeDtypeStruct((t, 1, d), F32),
        scratch_shapes=[pltpu.VMEM((bm, ka), BF16), pltpu.VMEM((bm, kb), BF16)],
        compiler_params=_cparams("arbitrary", "arbitrary"),
        name="even_out_proj",
    )(oa3, ob3, w_bf16, w_bf16, x3)


def _odd_in_kernel(x_ref, gain_ref, w_ref, o_ref, x2_scr, h_scr):
    @pl.when(pl.program_id(1) == 0)
    def _():
        h_scr[...] = _rms(_rows_2d(x_ref, x2_scr), gain_ref[...]).astype(BF16)

    acc = jnp.dot(h_scr[...], w_ref[...], preferred_element_type=F32)
    o_ref[...] = jax.nn.gelu(acc).astype(o_ref.dtype)


def _odd_in_proj(x3, gain, w_bf16, li):
    t, _, d = x3.shape
    n = w_bf16.shape[2]
    bm, bn = 512, 1024
    return pl.pallas_call(
        _odd_in_kernel,
        grid=(t // bm, n // bn),
        in_specs=[pl.BlockSpec((bm, 1, d), lambda i, j: (i, 0, 0)),
                  pl.BlockSpec((1, d), lambda i, j: (0, 0)),
                  pl.BlockSpec((None, d, bn), lambda i, j: (li, 0, j))],
        out_specs=pl.BlockSpec((bm, bn), lambda i, j: (i, j)),
        out_shape=jax.ShapeDtypeStruct((t, n), BF16),
        scratch_shapes=[pltpu.VMEM((bm, d), F32), pltpu.VMEM((bm, d), BF16)],
        compiler_params=_cparams("arbitrary", "arbitrary"),
        name="odd_in_proj",
    )(x3, gain[None, :], w_bf16)


def _odd_out_kernel(u_ref, v_ref, vg_ref, ws_ref, bcol_ref, w_ref, x_ref, o_ref, g_scr):
    @pl.when(pl.program_id(1) == 0)
    def _():
        bm, width = g_scr.shape
        gd = width // GM_GROUPS
        ssq = jnp.zeros((bm, 1), F32)
        for g in range(GM_GROUPS):
            vg = v_ref[:, g * gd:(g + 1) * gd].astype(F32)
            ssq = ssq + jnp.sum(vg * vg, axis=-1, keepdims=True)
        inv = lax.rsqrt(ssq / width + EPS)
        causal = (lax.broadcasted_iota(jnp.int32, (GM_CHUNK, GM_CHUNK), 0)
                  >= lax.broadcasted_iota(jnp.int32, (GM_CHUNK, GM_CHUNK), 1))
        wsum = jnp.where(causal, ws_ref[0], 0.0)
        for g in range(1, GM_GROUPS):
            wsum = wsum + jnp.where(causal, ws_ref[g], 0.0)
        wsum = wsum.astype(BF16)
        for g in range(GM_GROUPS):
            bcol = bcol_ref[:, g:g + 1]
            cols = slice(g * gd, (g + 1) * gd)
            for cc in range(bm // GM_CHUNK):
                rows = slice(cc * GM_CHUNK, (cc + 1) * GM_CHUNK)
                vn = (v_ref[rows, cols].astype(F32) * inv[rows, :] * vg_ref[:, cols]).astype(BF16)
                sv = jnp.dot(wsum, vn, preferred_element_type=F32) + bcol
                g_scr[rows, cols] = (u_ref[rows, cols].astype(F32) * sv).astype(BF16)

    mix = jnp.dot(g_scr[...], w_ref[...], preferred_element_type=F32)
    o_ref[...] = x_ref[...] + mix.reshape(o_ref.shape)


def _odd_out_proj(z, v_gain, w_spatial, b_spatial, w_bf16, li, x3):
    t, _, d = x3.shape
    width = z.shape[1] // 2
    bm, bn = 512, 1024
    return pl.pallas_call(
        _odd_out_kernel,
        grid=(t // bm, d // bn),
        in_specs=[pl.BlockSpec((bm, width), lambda i, j: (i, 0)),
                  pl.BlockSpec((bm, width), lambda i, j: (i, 1)),
                  pl.BlockSpec((1, width), lambda i, j: (0, 0)),
                  pl.BlockSpec((GM_GROUPS, GM_CHUNK, GM_CHUNK), lambda i, j: (0, 0, 0)),
                  pl.BlockSpec((GM_CHUNK, GM_GROUPS), lambda i, j: (0, 0)),
                  pl.BlockSpec((None, width, bn), lambda i, j: (li, 0, j)),
                  pl.BlockSpec((bm, 1, bn), lambda i, j: (i, 0, j))],
        out_specs=pl.BlockSpec((bm, 1, bn), lambda i, j: (i, 0, j)),
        out_shape=jax.ShapeDtypeStruct((t, 1, d), F32),
        scratch_shapes=[pltpu.VMEM((bm, width), BF16)],
        compiler_params=_cparams("arbitrary", "arbitrary"),
        name="odd_out_proj",
    )(z, z, v_gain[None, :], w_spatial, b_spatial.T, w_bf16, x3)


def _router_kernel(x_ref, gain_ref, w_ref, bias_ref, o_ref, x2_scr):
    h = _rms(_rows_2d(x_ref, x2_scr), gain_ref[...])
    logits = jnp.dot(h, w_ref[...], preferred_element_type=F32, precision=lax.Precision.HIGHEST)
    lt = logits.T[0:N_EXPERTS, :]
    e = jnp.exp(lt - jnp.max(lt, axis=0, keepdims=True))
    scores = e / jnp.sum(e, axis=0, keepdims=True)
    biased = scores + bias_ref[...]
    sc = [scores[i:i + 1, :] for i in range(N_EXPERTS)]
    bi = [biased[i:i + 1, :] for i in range(N_EXPERTS)]

    def top2_sum(a, b, c, d):
        return jnp.maximum(jnp.maximum(jnp.maximum(a + b, a + c), jnp.maximum(a + d, b + c)),
                           jnp.maximum(b + d, c + d))

    grp = [top2_sum(*bi[EXPERTS_PER_GROUP * g:EXPERTS_PER_GROUP * (g + 1)]) for g in range(N_EXPERT_GROUPS)]
    best = jnp.zeros_like(grp[0], dtype=jnp.int32)
    best_score = grp[0]
    for g in range(1, N_EXPERT_GROUPS):
        upd = grp[g] > best_score
        best = jnp.where(upd, g, best)
        best_score = jnp.where(upd, grp[g], best_score)

    def pick(vals, k):
        out = vals[k]
        for g in range(1, N_EXPERT_GROUPS):
            out = jnp.where(best == g, vals[EXPERTS_PER_GROUP * g + k], out)
        return out

    vb = [pick(bi, k) for k in range(EXPERTS_PER_GROUP)]
    vs = [pick(sc, k) for k in range(EXPERTS_PER_GROUP)]
    i1 = jnp.zeros_like(best)
    m1 = vb[0]
    for k in range(1, EXPERTS_PER_GROUP):
        upd = vb[k] > m1
        i1 = jnp.where(upd, k, i1)
        m1 = jnp.where(upd, vb[k], m1)
    i2 = jnp.zeros_like(best)
    m2 = jnp.full_like(m1, -jnp.inf)
    for k in range(EXPERTS_PER_GROUP):
        upd = (i1 != k) & (vb[k] > m2)
        i2 = jnp.where(upd, k, i2)
        m2 = jnp.where(upd, vb[k], m2)

    def take(idx):
        out = vs[0]
        for k in range(1, EXPERTS_PER_GROUP):
            out = jnp.where(idx == k, vs[k], out)
        return out

    g1, g2 = take(i1), take(i2)
    tot = g1 + g2
    g1, g2 = g1 / tot, g2 / tot
    swap = i2 < i1
    lo = jnp.where(swap, i2, i1)
    hi = jnp.where(swap, i1, i2)
    o_ref[0:1, :] = best.astype(F32)
    o_ref[1:2, :] = lo.astype(F32)
    o_ref[2:3, :] = hi.astype(F32)
    o_ref[3:4, :] = jnp.where(swap, g2, g1)
    o_ref[4:5, :] = jnp.where(swap, g1, g2)
    o_ref[5:8, :] = jnp.zeros((3, best.shape[1]), F32)


def _router(x3, gain, router_w_pad, router_bias):
    t, _, d = x3.shape
    bm = 512
    return pl.pallas_call(
        _router_kernel,
        grid=(t // bm,),
        in_specs=[pl.BlockSpec((bm, 1, d), lambda i: (i, 0, 0)),
                  pl.BlockSpec((1, d), lambda i: (0, 0)),
                  pl.BlockSpec((d, HEAD_DIM), lambda i: (0, 0)),
                  pl.BlockSpec((N_EXPERTS, 1), lambda i: (0, 0))],
        out_specs=pl.BlockSpec((8, bm), lambda i: (0, i)),
        out_shape=jax.ShapeDtypeStruct((8, t), F32),
        scratch_shapes=[pltpu.VMEM((bm, d), F32)],
        compiler_params=_cparams("arbitrary"),
        name="moe_router",
    )(x3, gain[None, :], router_w_pad, router_bias[:, None])


def _moe_plan(route, n_tiles):
    t = route.shape[1]
    i32 = jnp.int32
    best, lo, hi = route[0].astype(i32), route[1].astype(i32), route[2].astype(i32)
    pair = lo * (7 - lo) // 2 + (hi - lo - 1)
    bucket = best * N_PAIRS + pair
    onehot = bucket[:, None] == jnp.arange(N_BUCKETS, dtype=i32)[None, :]
    csum = jnp.cumsum(onehot.astype(i32), axis=0)
    counts = csum[-1]
    tiles = (counts + MOE_TILE - 1) // MOE_TILE
    tile_end = jnp.cumsum(tiles)
    tile_start = tile_end - tiles
    total = tile_end[-1]
    pos = jnp.sum(jnp.where(onehot, csum - 1 + tile_start[None, :] * MOE_TILE, 0), axis=1)
    packed = jnp.stack([jnp.arange(t, dtype=F32), route[3], route[4]], axis=1)
    init = jnp.broadcast_to(jnp.asarray([-1.0, 0.0, 0.0], F32), (n_tiles * MOE_TILE, 3))
    rows = init.at[pos].set(packed, unique_indices=True)
    token_of = rows[:, 0].astype(i32)
    gates = rows[:, 1:3]
    tile_ids = jnp.arange(n_tiles, dtype=i32)
    tile_valid = (tile_ids < total).astype(i32)
    tile_bucket = jnp.sum(tile_end[None, :] <= jnp.minimum(tile_ids, total - 1)[:, None], axis=1).astype(i32)
    tile_group = tile_bucket // N_PAIRS
    tile_pair = tile_bucket % N_PAIRS
    pair_lo = (tile_pair >= 3).astype(i32) + (tile_pair >= 5).astype(i32)
    pair_hi = tile_pair - pair_lo * (7 - pair_lo) // 2 + pair_lo + 1
    odd = (tile_ids & 1) == 1
    first = tile_group * EXPERTS_PER_GROUP + jnp.where(odd, pair_hi, pair_lo)
    second = tile_group * EXPERTS_PER_GROUP + jnp.where(odd, pair_lo, pair_hi)
    step_expert = jnp.stack([first, second], axis=1).reshape(-1).astype(i32)
    last = jnp.sum(jnp.where(jnp.arange(2 * n_tiles, dtype=i32) == 2 * total - 1, step_expert, 0))
    step_expert = jnp.where(jnp.repeat(tile_valid, 2) > 0, step_expert, last)
    return step_expert, tile_valid, token_of, gates


def _moe_kernel(se_ref, tv_ref, tok_ref, x_hbm, gates_ref, gain_ref, wg_ref, wu_ref, wd_ref,
                out_hbm, xbuf, x2_scr, hbuf, obuf, sem_in, sem_out):
    del se_ref
    n_tiles = pl.num_programs(0)
    i = pl.program_id(0)
    s = pl.program_id(1)
    valid = tv_ref[i] > 0
    next_valid = tv_ref[jnp.minimum(i + 1, n_tiles - 1)] * (i + 1 < n_tiles) > 0

    def row_in(r, tok):
        return pltpu.make_async_copy(x_hbm.at[pl.ds(tok, 1)], xbuf.at[pl.ds(r, 1)], sem_in)

    def row_out(r, tok):
        return pltpu.make_async_copy(obuf.at[pl.ds(r, 1)], out_hbm.at[pl.ds(tok, 1)], sem_out)

    def gather_start(tile):
        def body(r, carry):
            row_in(r, jnp.maximum(tok_ref[tile * MOE_TILE + r], 0)).start()
            return carry
        lax.fori_loop(0, MOE_TILE, body, 0, unroll=8)

    def gather_wait():
        def body(r, carry):
            row_in(r, 0).wait()
            return carry
        lax.fori_loop(0, MOE_TILE, body, 0, unroll=8)

    def scatter_start(tile):
        def body(r, carry):
            tok = tok_ref[tile * MOE_TILE + r]

            @pl.when(tok >= 0)
            def _():
                row_out(r, tok).start()
            return carry
        lax.fori_loop(0, MOE_TILE, body, 0, unroll=8)

    def scatter_wait(tile):
        def body(r, carry):
            @pl.when(tok_ref[tile * MOE_TILE + r] >= 0)
            def _():
                row_out(r, 0).wait()
            return carry
        lax.fori_loop(0, MOE_TILE, body, 0, unroll=8)

    @pl.when(valid & (s == 0))
    def _():
        @pl.when(i == 0)
        def _():
            gather_start(0)

        gather_wait()
        x = _rows_2d(xbuf, x2_scr)
        hbuf[...] = _rms(x, gain_ref[...]).astype(BF16)

        @pl.when(next_valid)
        def _():
            gather_start(i + 1)

    @pl.when(valid)
    def _():
        which = s ^ (i & 1)
        gate = jnp.where(which == 0, gates_ref[:, 0:1], gates_ref[:, 1:2])
        h = hbuf[...]
        hg = jnp.dot(h, wg_ref[...], preferred_element_type=F32)
        hu = jnp.dot(h, wu_ref[...], preferred_element_type=F32)
        act = (jax.nn.silu(hg) * hu * gate).astype(BF16)
        y = jnp.dot(act, wd_ref[...], preferred_element_type=F32)

        @pl.when(s == 0)
        def _():
            x2_scr[...] = x2_scr[...] + y

        @pl.when(s == 1)
        def _():
            @pl.when(i > 0)
            def _():
                scatter_wait(i - 1)

            obuf[...] = (x2_scr[...] + y).reshape(obuf.shape)
            scatter_start(i)

            @pl.when(jnp.logical_not(next_valid))
            def _():
                scatter_wait(i)


def _grouped_moe(x3, gain, router_w_pad, router_bias, wg_bf16, wu_bf16, wd_bf16, li):
    t, _, d = x3.shape
    f = wg_bf16.shape[3]
    n_tiles = t // MOE_TILE + N_BUCKETS
    route = _router(x3, gain, router_w_pad, router_bias)
    step_expert, tile_valid, token_of, gates = _moe_plan(route, n_tiles)
    grid_spec = pltpu.PrefetchScalarGridSpec(
        num_scalar_prefetch=3,
        grid=(n_tiles, 2),
        in_specs=[pl.BlockSpec(memory_space=pl.ANY),
                  pl.BlockSpec((MOE_TILE, 2), lambda i, s, se, tv, tok: (i, 0)),
                  pl.BlockSpec((1, d), lambda i, s, se, tv, tok: (0, 0)),
                  pl.BlockSpec((None, None, d, f), lambda i, s, se, tv, tok: (li, se[2 * i + s], 0, 0)),
                  pl.BlockSpec((None, None, d, f), lambda i, s, se, tv, tok: (li, se[2 * i + s], 0, 0)),
                  pl.BlockSpec((None, None, f, d), lambda i, s, se, tv, tok: (li, se[2 * i + s], 0, 0))],
        out_specs=pl.BlockSpec(memory_space=pl.ANY),
        scratch_shapes=[pltpu.VMEM((MOE_TILE, 1, d), F32),
                        pltpu.VMEM((MOE_TILE, d), F32),
                        pltpu.VMEM((MOE_TILE, d), BF16),
                        pltpu.VMEM((MOE_TILE, 1, d), F32),
                        pltpu.SemaphoreType.DMA(()),
                        pltpu.SemaphoreType.DMA(())],
    )
    return pl.pallas_call(
        _moe_kernel,
        grid_spec=grid_spec,
        out_shape=jax.ShapeDtypeStruct((t, 1, d), F32),
        compiler_params=_cparams("arbitrary", "arbitrary"),
        name="moe_experts",
    )(step_expert, tile_valid, token_of, x3, gates, gain[None, :], wg_bf16, wu_bf16, wd_bf16)


def kernel(x, positions, norm_mix, norm_ffn, a_w_in, a_q_norm, a_k_norm, b_w_group, b_scale, ab_w_out,
           c_w_in, c_v_norm, c_w_spatial, c_b_spatial, c_w_out, router_w, router_bias,
           expert_w_gate, expert_w_up, expert_w_down):
    batch, t, d = x.shape
    assert batch == 1 and t % (N_STREAM * ATTN_BLOCK) == 0
    depth = norm_mix.shape[0]
    a_width = (a_w_in.shape[2] - b_w_group.shape[1] * b_w_group.shape[2]) // 3
    n_heads = a_width // HEAD_DIM
    ns = t // N_STREAM

    x3 = x.reshape(t, 1, d)
    pos_rows = positions[0].reshape(ns, N_STREAM).T.reshape(t)
    cos, sin = (tab.reshape(N_STREAM, ns, HEAD_DIM) for tab in _trig_tables(pos_rows))
    router_w_pad = jnp.pad(router_w, ((0, 0), (0, HEAD_DIM - N_EXPERTS)))
    a_w_in, b_w_group, ab_w_out, c_w_in, c_w_out, expert_w_gate, expert_w_up, expert_w_down = (
        w.astype(BF16) for w in (a_w_in, b_w_group, ab_w_out, c_w_in, c_w_out,
                                 expert_w_gate, expert_w_up, expert_w_down))

    for layer in range(depth):
        i = layer // 2
        if layer % 2 == 0:
            proj3 = _even_in_proj(x3, norm_mix[layer], a_w_in, i,
                                  jnp.stack([a_q_norm[i], a_k_norm[i]]), cos, sin, a_width)
            oa3 = _dilated_attention(proj3, n_heads)
            ob3 = _pooling_mixer(proj3, b_w_group, i, b_scale[i])
            x3 = _even_out_proj(oa3, ob3, ab_w_out, i, x3)
        else:
            z = _odd_in_proj(x3, norm_mix[layer], c_w_in, i)
            x3 = _odd_out_proj(z, c_v_norm[i], c_w_spatial[i], c_b_spatial[i], c_w_out, i, x3)
        x3 = _grouped_moe(x3, norm_ffn[layer], router_w_pad, router_bias,
                          expert_w_gate, expert_w_up, expert_w_down, layer)
    return x3.reshape(1, t, d)
```

```python
import functools

import jax
import jax.numpy as jnp
from jax import lax
from jax.experimental import pallas as pl
from jax.experimental.pallas import tpu as pltpu

F32 = jnp.float32
BF16 = jnp.bfloat16

EPS = 1e-6
HEAD_DIM = 128
ROPE_DIM = HEAD_DIM // 4
ROPE_HALF = ROPE_DIM // 2
ROPE_THETA = 500000.0
POOL_SIZES = (2, 4, 8, 16)
N_STREAM = 16
ATTN_BLOCK = 128
DILATIONS = (16, 4, 1)
GM_GROUPS = 8
GM_CHUNK = 128
N_EXPERTS = 16
N_EXPERT_GROUPS = 4
EXPERTS_PER_GROUP = 4
N_PAIRS = 6
N_BUCKETS = N_EXPERT_GROUPS * N_PAIRS
MOE_TILE = 256
V7X_VMEM_LIMIT = 56 * 1024 * 1024


def _cparams(*sem):
    return pltpu.CompilerParams(dimension_semantics=sem, vmem_limit_bytes=V7X_VMEM_LIMIT)


def _rms(xf, gain_row):
    ms = jnp.mean(xf * xf, axis=-1, keepdims=True)
    return xf * lax.rsqrt(ms + EPS) * gain_row


def _rows_2d(x_ref, x2_scr):
    if len(x_ref.shape) == 2:
        return x_ref[...]
    x2_scr[...] = x_ref[...].reshape(x2_scr.shape)
    return x2_scr[...]


def _x_spec(x, bm, bn, index):
    if x.ndim == 2:
        return pl.BlockSpec((bm, bn), lambda *g: index(*g))
    return pl.BlockSpec((bm, 1, bn), lambda *g: (index(*g)[0], 0, index(*g)[1]))


def _trig_kernel(pos_ref, freq_ref, cos_ref, sin_ref):
    ang = pos_ref[...].astype(F32) * freq_ref[...]
    lane = lax.broadcasted_iota(jnp.int32, ang.shape, 1)
    c = jnp.cos(ang)
    s = jnp.sin(ang)
    cos_ref[...] = jnp.where(lane < ROPE_DIM, c, 1.0)
    sin_ref[...] = jnp.where(lane < ROPE_HALF, -s, jnp.where(lane < ROPE_DIM, s, 0.0))


def _trig_tables(pos_rows):
    t = pos_rows.shape[0]
    inv_freq = ROPE_THETA ** (-jnp.arange(ROPE_HALF, dtype=F32) / ROPE_HALF)
    freq = jnp.tile(inv_freq, HEAD_DIM // ROPE_HALF)[None, :]
    pos_b = jnp.broadcast_to(pos_rows[:, None], (t, HEAD_DIM))
    bm = 1024
    return pl.pallas_call(
        _trig_kernel,
        grid=(t // bm,),
        in_specs=[pl.BlockSpec((bm, HEAD_DIM), lambda i: (i, 0)),
                  pl.BlockSpec((1, HEAD_DIM), lambda i: (0, 0))],
        out_specs=[pl.BlockSpec((bm, HEAD_DIM), lambda i: (i, 0))] * 2,
        out_shape=[jax.ShapeDtypeStruct((t, HEAD_DIM), F32)] * 2,
        compiler_params=_cparams("arbitrary"),
        name="rope_tables",
    )(pos_b, freq)


def _even_in_kernel(x_ref, gain_ref, w_ref, qkg_ref, cos_ref, sin_ref, o_ref, x2_scr, h_scr, *, n_qk_blocks):
    j = pl.program_id(1)

    bs = o_ref.shape[1]
    bm = N_STREAM * bs

    @pl.when(j == 0)
    def _():
        i = lax.broadcasted_iota(jnp.int32, (bm, bm), 0)
        c = lax.broadcasted_iota(jnp.int32, (bm, bm), 1)
        perm = jnp.where(c == (i % bs) * N_STREAM + i // bs, 1.0, 0.0).astype(BF16)
        h = _rms(_rows_2d(x_ref, x2_scr), gain_ref[...]).astype(BF16)
        h_scr[...] = jnp.dot(perm, h, preferred_element_type=F32).astype(BF16)

    acc = jnp.dot(h_scr[...], w_ref[...], preferred_element_type=F32)

    @pl.when(j < n_qk_blocks)
    def _():
        gain = jnp.where(j < n_qk_blocks // 2, qkg_ref[0:1, :], qkg_ref[1:2, :])
        cos = cos_ref[...].reshape(bm, HEAD_DIM)
        sin = sin_ref[...].reshape(bm, HEAD_DIM)
        lane = lax.broadcasted_iota(jnp.int32, cos.shape, 1)
        for hh in range(acc.shape[1] // HEAD_DIM):
            y = _rms(acc[:, hh * HEAD_DIM:(hh + 1) * HEAD_DIM], gain)
            swapped = jnp.where(lane < ROPE_HALF,
                                pltpu.roll(y, HEAD_DIM - ROPE_HALF, 1),
                                pltpu.roll(y, ROPE_HALF, 1))
            o_ref[:, :, hh * HEAD_DIM:(hh + 1) * HEAD_DIM] = (y * cos + swapped * sin).reshape(N_STREAM, bs, HEAD_DIM)

    @pl.when(j >= n_qk_blocks)
    def _():
        o_ref[...] = acc.reshape(o_ref.shape)


def _even_in_proj(x, t, gain, w_bf16, li, qk_gain, cos3, sin3, a_width):
    d = x.shape[-1]
    n = w_bf16.shape[2]
    ns = t // N_STREAM
    bm, bn = 512, 512
    bs = bm // N_STREAM
    kern = functools.partial(_even_in_kernel, n_qk_blocks=2 * a_width // bn)
    return pl.pallas_call(
        kern,
        grid=(t // bm, n // bn),
        in_specs=[_x_spec(x, bm, d, lambda i, j: (i, 0)),
                  pl.BlockSpec((1, d), lambda i, j: (0, 0)),
                  pl.BlockSpec((None, d, bn), lambda i, j: (li, 0, j)),
                  pl.BlockSpec((2, HEAD_DIM), lambda i, j: (0, 0)),
                  pl.BlockSpec((N_STREAM, bs, HEAD_DIM), lambda i, j: (0, i, 0)),
                  pl.BlockSpec((N_STREAM, bs, HEAD_DIM), lambda i, j: (0, i, 0))],
        out_specs=pl.BlockSpec((N_STREAM, bs, bn), lambda i, j: (0, i, j)),
        out_shape=jax.ShapeDtypeStruct((N_STREAM, ns, n), F32),
        scratch_shapes=[pltpu.VMEM((bm, d), F32), pltpu.VMEM((bm, d), BF16)],
        compiler_params=_cparams("arbitrary", "arbitrary"),
        name="even_in_proj",
    )(x, gain[None, :], w_bf16, qk_gain, cos3, sin3)


def _attn_kernel(q_ref, k_ref, v_ref, o_ref, o_scr, l_scr):
    sb = pl.program_id(1)
    scale = HEAD_DIM ** -0.5
    base = pl.multiple_of(sb * ATTN_BLOCK, ATTN_BLOCK)
    prev_base = pl.multiple_of(jnp.maximum(base - ATTN_BLOCK, 0), ATTN_BLOCK)
    row = lax.broadcasted_iota(jnp.int32, (ATTN_BLOCK, 2 * ATTN_BLOCK), 0)
    col = lax.broadcasted_iota(jnp.int32, (ATTN_BLOCK, 2 * ATTN_BLOCK), 1)
    is_prev = col < ATTN_BLOCK
    colk = col & (ATTN_BLOCK - 1)

    for pat, d in enumerate(DILATIONS):
        c = N_STREAM // d
        cl = ATTN_BLOCK // c
        sh = cl.bit_length() - 1
        qpos = c * (row & (cl - 1)) + (row >> sh)
        kpos = c * (colk & (cl - 1)) + (colk >> sh)
        bias = jnp.where(is_prev,
                         jnp.where(qpos <= kpos, 0.0, -jnp.inf),
                         jnp.where(qpos >= kpos, 0.0, -jnp.inf)).astype(F32)
        bias_start = jnp.where(jnp.logical_and(is_prev, sb == 0), -jnp.inf, bias)

        for r_d in range(d):
            for jj in range(c):
                lo = jj * cl
                rows = [r_d + d * a for a in range(c)]
                if jj == 0:
                    prev = pl.ds(prev_base + (ATTN_BLOCK - cl), cl)
                else:
                    prev = pl.ds(base + (lo - cl), cl)
                cur = pl.ds(base + lo, cl)
                q = jnp.concatenate([q_ref[r, lo:lo + cl, :] for r in rows], axis=0).astype(BF16)
                k = jnp.concatenate([k_ref[r, prev, :] for r in rows] + [k_ref[r, cur, :] for r in rows],
                                    axis=0).astype(BF16)
                v = jnp.concatenate([v_ref[r, prev, :] for r in rows] + [v_ref[r, cur, :] for r in rows],
                                    axis=0).astype(BF16)
                s = lax.dot_general(q, k, (((1,), (1,)), ((), ())), preferred_element_type=F32) * scale
                s = s + (bias_start if jj == 0 else bias)
                m = jnp.max(s, axis=-1, keepdims=True)
                p = jnp.exp(s - m)
                den = jnp.sum(p, axis=-1, keepdims=True)
                o = jnp.dot(p.astype(BF16), v, preferred_element_type=F32) / den
                lse = jnp.broadcast_to(m + jnp.log(den), (ATTN_BLOCK, HEAD_DIM))
                for a, r in enumerate(rows):
                    o_scr[pat, r, lo:lo + cl, :] = o[a * cl:(a + 1) * cl, :]
                    l_scr[pat, r, lo:lo + cl, :] = lse[a * cl:(a + 1) * cl, :]

    def mix(r, carry):
        l0, l1, l2 = l_scr[0, r], l_scr[1, r], l_scr[2, r]
        mx = jnp.maximum(jnp.maximum(l0, l1), l2)
        w0, w1, w2 = jnp.exp(l0 - mx), jnp.exp(l1 - mx), jnp.exp(l2 - mx)
        num = w0 * o_scr[0, r] + w1 * o_scr[1, r] + w2 * o_scr[2, r]
        o_ref[r] = (num / (w0 + w1 + w2)).astype(o_ref.dtype)
        return carry

    lax.fori_loop(0, N_STREAM, mix, 0)


def _dilated_attention(proj3, n_heads):
    _, ns, n = proj3.shape
    n_sb = ns // ATTN_BLOCK
    out3 = pl.pallas_call(
        _attn_kernel,
        grid=(n_heads, n_sb),
        in_specs=[pl.BlockSpec((N_STREAM, ATTN_BLOCK, HEAD_DIM), lambda h, sb: (0, sb, h)),
                  pl.BlockSpec((N_STREAM, ns, HEAD_DIM), lambda h, sb: (0, 0, n_heads + h)),
                  pl.BlockSpec((N_STREAM, ns, HEAD_DIM), lambda h, sb: (0, 0, 2 * n_heads + h))],
        out_specs=pl.BlockSpec((N_STREAM, ATTN_BLOCK, HEAD_DIM), lambda h, sb: (0, sb, h)),
        out_shape=jax.ShapeDtypeStruct((N_STREAM, ns, n_heads * HEAD_DIM), BF16),
        scratch_shapes=[pltpu.VMEM((3, N_STREAM, ATTN_BLOCK, HEAD_DIM), F32),
                        pltpu.VMEM((3, N_STREAM, ATTN_BLOCK, HEAD_DIM), F32)],
        compiler_params=_cparams("arbitrary", "arbitrary"),
        name="dilated_attention",
    )(proj3, proj3, proj3)
    return out3


def _pool_kernel(pb_ref, w_ref, sc_ref, o_ref, pre_scr):
    g = pl.program_id(0)
    ns, gw = pb_ref.shape[1], pb_ref.shape[2]
    first = lax.broadcasted_iota(jnp.int32, (ns, gw), 0) == 0

    for r in range(N_STREAM):
        pre_scr[r] = pb_ref[r] if r == 0 else pre_scr[r - 1] + pb_ref[r]

    for gi, p in enumerate(POOL_SIZES):
        @pl.when(g == gi)
        def _(p=p):
            for r in range(N_STREAM):
                win = pre_scr[r] - pre_scr[r - p] if r - p >= 0 else pre_scr[r]
                if r - p + 1 < 0:
                    wrap = pre_scr[N_STREAM - 1] - pre_scr[r - p + N_STREAM]
                    win = win + jnp.where(first, 0.0, pltpu.roll(wrap, 1, 0))
                cnt = jnp.where(first, float(min(r + 1, p)), float(p))
                pooled = win / cnt - pb_ref[r]
                y = jnp.dot(pooled.astype(BF16), w_ref[...], preferred_element_type=F32) * sc_ref[...]
                o_ref[r] = y.astype(o_ref.dtype)


def _pooling_mixer(proj3, w_group_bf16, li, scale):
    _, ns, n = proj3.shape
    _, n_groups, gw, _ = w_group_bf16.shape
    pb_blk0 = (n - n_groups * gw) // gw
    return pl.pallas_call(
        _pool_kernel,
        grid=(n_groups,),
        in_specs=[pl.BlockSpec((N_STREAM, ns, gw), lambda g: (0, 0, pb_blk0 + g)),
                  pl.BlockSpec((None, None, gw, gw), lambda g: (li, g, 0, 0)),
                  pl.BlockSpec((1, gw), lambda g: (0, g))],
        out_specs=pl.BlockSpec((N_STREAM, ns, gw), lambda g: (0, 0, g)),
        out_shape=jax.ShapeDtypeStruct((N_STREAM, ns, n_groups * gw), BF16),
        scratch_shapes=[pltpu.VMEM((N_STREAM, ns, gw), F32)],
        compiler_params=_cparams("arbitrary"),
        name="pooling_mixer",
    )(proj3, w_group_bf16, scale[None, :])


def _even_out_kernel(a_ref, b_ref, wa_ref, wb_ref, x_ref, fg_ref, rhi_ref, rlo_ref, rb_ref,
                     o_ref, route_ref, a_scr, b_scr, x2_scr, lg_scr, ss_scr, *, width):
    bs = a_ref.shape[1]
    bm = N_STREAM * bs

    @pl.when(pl.program_id(1) == 0)
    def _():
        n = lax.broadcasted_iota(jnp.int32, (bm, bm), 0)
        c = lax.broadcasted_iota(jnp.int32, (bm, bm), 1)
        perm = jnp.where(c == (n % N_STREAM) * bs + n // N_STREAM, 1.0, 0.0).astype(BF16)
        a = a_ref[...].reshape(bm, a_ref.shape[2])
        b = b_ref[...].reshape(bm, b_ref.shape[2])
        a_scr[...] = jnp.dot(perm, a, preferred_element_type=F32).astype(BF16)
        b_scr[...] = jnp.dot(perm, b, preferred_element_type=F32).astype(BF16)

    mix = (jnp.dot(a_scr[...], wa_ref[...], preferred_element_type=F32)
           + jnp.dot(b_scr[...], wb_ref[...], preferred_element_type=F32))
    xn = _rows_2d(x_ref, x2_scr) + mix
    o_ref[...] = xn.reshape(o_ref.shape)
    _router_accumulate(xn, fg_ref, rhi_ref, rlo_ref, rb_ref, route_ref, lg_scr, ss_scr, width)


def _even_out_proj(oa3, ob3, w_bf16, li, x, t, router):
    d = x.shape[-1]
    ka, kb = oa3.shape[2], ob3.shape[2]
    assert ka % kb == 0
    bm, bn = 512, 1024
    bs = bm // N_STREAM
    r_in, r_out, r_scr = _router_specs(bm, bn)
    return pl.pallas_call(
        functools.partial(_even_out_kernel, width=d),
        grid=(t // bm, d // bn),
        in_specs=[pl.BlockSpec((N_STREAM, bs, ka), lambda i, j: (0, i, 0)),
                  pl.BlockSpec((N_STREAM, bs, kb), lambda i, j: (0, i, 0)),
                  pl.BlockSpec((None, ka, bn), lambda i, j: (li, 0, j)),
                  pl.BlockSpec((None, kb, bn), lambda i, j: (li, ka // kb, j)),
                  _x_spec(x, bm, bn, lambda i, j: (i, j))] + r_in,
        out_specs=[pl.BlockSpec((bm, 1, bn), lambda i, j: (i, 0, j)), r_out],
        out_shape=[jax.ShapeDtypeStruct((t, 1, d), F32), jax.ShapeDtypeStruct((8, t), F32)],
        scratch_shapes=[pltpu.VMEM((bm, ka), BF16), pltpu.VMEM((bm, kb), BF16), pltpu.VMEM((bm, bn), F32)] + r_scr,
        compiler_params=_cparams("arbitrary", "arbitrary"),
        name="even_out_proj",
    )(oa3, ob3, w_bf16, w_bf16, x, *router)


def _odd_in_kernel(x_ref, gain_ref, w_ref, o_ref, x2_scr, h_scr):
    @pl.when(pl.program_id(1) == 0)
    def _():
        h_scr[...] = _rms(_rows_2d(x_ref, x2_scr), gain_ref[...]).astype(BF16)

    acc = jnp.dot(h_scr[...], w_ref[...], preferred_element_type=F32)
    o_ref[...] = jax.nn.gelu(acc).astype(o_ref.dtype)


def _odd_in_proj(x, t, gain, w_bf16, li):
    d = x.shape[-1]
    n = w_bf16.shape[2]
    bm, bn = 512, 1024
    return pl.pallas_call(
        _odd_in_kernel,
        grid=(t // bm, n // bn),
        in_specs=[_x_spec(x, bm, d, lambda i, j: (i, 0)),
                  pl.BlockSpec((1, d), lambda i, j: (0, 0)),
                  pl.BlockSpec((None, d, bn), lambda i, j: (li, 0, j))],
        out_specs=pl.BlockSpec((bm, bn), lambda i, j: (i, j)),
        out_shape=jax.ShapeDtypeStruct((t, n), BF16),
        scratch_shapes=[pltpu.VMEM((bm, d), F32), pltpu.VMEM((bm, d), BF16)],
        compiler_params=_cparams("arbitrary", "arbitrary"),
        name="odd_in_proj",
    )(x, gain[None, :], w_bf16)


def _odd_out_kernel(u_ref, v_ref, vg_ref, ws_ref, bcol_ref, w_ref, x_ref, fg_ref, rhi_ref, rlo_ref, rb_ref,
                    o_ref, route_ref, g_scr, x2_scr, lg_scr, ss_scr):
    @pl.when(pl.program_id(1) == 0)
    def _():
        bm, width = g_scr.shape
        gd = width // GM_GROUPS
        ssq = jnp.zeros((bm, 1), F32)
        for g in range(GM_GROUPS):
            vg = v_ref[:, g * gd:(g + 1) * gd].astype(F32)
            ssq = ssq + jnp.sum(vg * vg, axis=-1, keepdims=True)
        inv = lax.rsqrt(ssq / width + EPS)
        causal = (lax.broadcasted_iota(jnp.int32, (GM_CHUNK, GM_CHUNK), 0)
                  >= lax.broadcasted_iota(jnp.int32, (GM_CHUNK, GM_CHUNK), 1))
        wsum = jnp.where(causal, ws_ref[0], 0.0)
        for g in range(1, GM_GROUPS):
            wsum = wsum + jnp.where(causal, ws_ref[g], 0.0)
        wsum = wsum.astype(BF16)
        for g in range(GM_GROUPS):
            bcol = bcol_ref[:, g:g + 1]
            cols = slice(g * gd, (g + 1) * gd)
            for cc in range(bm // GM_CHUNK):
                rows = slice(cc * GM_CHUNK, (cc + 1) * GM_CHUNK)
                vn = (v_ref[rows, cols].astype(F32) * inv[rows, :] * vg_ref[:, cols]).astype(BF16)
                sv = jnp.dot(wsum, vn, preferred_element_type=F32) + bcol
                g_scr[rows, cols] = (u_ref[rows, cols].astype(F32) * sv).astype(BF16)

    xn = _rows_2d(x_ref, x2_scr) + jnp.dot(g_scr[...], w_ref[...], preferred_element_type=F32)
    o_ref[...] = xn.reshape(o_ref.shape)
    _router_accumulate(xn, fg_ref, rhi_ref, rlo_ref, rb_ref, route_ref, lg_scr, ss_scr, g_scr.shape[1])


def _odd_out_proj(z, v_gain, w_spatial, b_spatial, w_bf16, li, x, t, router):
    d = x.shape[-1]
    width = z.shape[1] // 2
    assert width == d
    bm, bn = 512, 1024
    r_in, r_out, r_scr = _router_specs(bm, bn)
    return pl.pallas_call(
        _odd_out_kernel,
        grid=(t // bm, d // bn),
        in_specs=[pl.BlockSpec((bm, width), lambda i, j: (i, 0)),
                  pl.BlockSpec((bm, width), lambda i, j: (i, 1)),
                  pl.BlockSpec((1, width), lambda i, j: (0, 0)),
                  pl.BlockSpec((GM_GROUPS, GM_CHUNK, GM_CHUNK), lambda i, j: (0, 0, 0)),
                  pl.BlockSpec((GM_CHUNK, GM_GROUPS), lambda i, j: (0, 0)),
                  pl.BlockSpec((None, width, bn), lambda i, j: (li, 0, j)),
                  _x_spec(x, bm, bn, lambda i, j: (i, j))] + r_in,
        out_specs=[pl.BlockSpec((bm, 1, bn), lambda i, j: (i, 0, j)), r_out],
        out_shape=[jax.ShapeDtypeStruct((t, 1, d), F32), jax.ShapeDtypeStruct((8, t), F32)],
        scratch_shapes=[pltpu.VMEM((bm, width), BF16), pltpu.VMEM((bm, bn), F32)] + r_scr,
        compiler_params=_cparams("arbitrary", "arbitrary"),
        name="odd_out_proj",
    )(z, z, v_gain[None, :], w_spatial, b_spatial.T, w_bf16, x, *router)


def _router_accumulate(xn, gain_ref, whi_ref, wlo_ref, bias_ref, route_ref, lg_scr, ss_scr, width):
    j = pl.program_id(1)
    xg = xn * gain_ref[...]
    hi = xg.astype(BF16)
    lo = (xg - hi.astype(F32)).astype(BF16)
    part = (jnp.dot(hi, whi_ref[...], preferred_element_type=F32)
            + jnp.dot(lo, whi_ref[...], preferred_element_type=F32)
            + jnp.dot(hi, wlo_ref[...], preferred_element_type=F32))
    ssq = jnp.sum(xn * xn, axis=-1, keepdims=True)

    @pl.when(j == 0)
    def _():
        lg_scr[...] = part
        ss_scr[...] = ssq

    @pl.when(j > 0)
    def _():
        lg_scr[...] = lg_scr[...] + part
        ss_scr[...] = ss_scr[...] + ssq

    @pl.when(j == pl.num_programs(1) - 1)
    def _():
        _route_rows(lg_scr[...] * lax.rsqrt(ss_scr[...] / width + EPS), bias_ref, route_ref)


def _route_rows(logits, bias_ref, o_ref):
    lt = logits.T[0:N_EXPERTS, :]
    e = jnp.exp(lt - jnp.max(lt, axis=0, keepdims=True))
    scores = e / jnp.sum(e, axis=0, keepdims=True)
    biased = scores + bias_ref[...]
    sc = [scores[i:i + 1, :] for i in range(N_EXPERTS)]
    bi = [biased[i:i + 1, :] for i in range(N_EXPERTS)]

    def top2_sum(a, b, c, d):
        return jnp.maximum(jnp.maximum(jnp.maximum(a + b, a + c), jnp.maximum(a + d, b + c)),
                           jnp.maximum(b + d, c + d))

    grp = [top2_sum(*bi[EXPERTS_PER_GROUP * g:EXPERTS_PER_GROUP * (g + 1)]) for g in range(N_EXPERT_GROUPS)]
    best = jnp.zeros_like(grp[0], dtype=jnp.int32)
    best_score = grp[0]
    for g in range(1, N_EXPERT_GROUPS):
        upd = grp[g] > best_score
        best = jnp.where(upd, g, best)
        best_score = jnp.where(upd, grp[g], best_score)

    def pick(vals, k):
        out = vals[k]
        for g in range(1, N_EXPERT_GROUPS):
            out = jnp.where(best == g, vals[EXPERTS_PER_GROUP * g + k], out)
        return out

    vb = [pick(bi, k) for k in range(EXPERTS_PER_GROUP)]
    vs = [pick(sc, k) for k in range(EXPERTS_PER_GROUP)]
    i1 = jnp.zeros_like(best)
    m1 = vb[0]
    for k in range(1, EXPERTS_PER_GROUP):
        upd = vb[k] > m1
        i1 = jnp.where(upd, k, i1)
        m1 = jnp.where(upd, vb[k], m1)
    i2 = jnp.zeros_like(best)
    m2 = jnp.full_like(m1, -jnp.inf)
    for k in range(EXPERTS_PER_GROUP):
        upd = (i1 != k) & (vb[k] > m2)
        i2 = jnp.where(upd, k, i2)
        m2 = jnp.where(upd, vb[k], m2)

    def take(idx):
        out = vs[0]
        for k in range(1, EXPERTS_PER_GROUP):
            out = jnp.where(idx == k, vs[k], out)
        return out

    g1, g2 = take(i1), take(i2)
    tot = g1 + g2
    g1, g2 = g1 / tot, g2 / tot
    swap = i2 < i1
    lo = jnp.where(swap, i2, i1)
    hi = jnp.where(swap, i1, i2)
    o_ref[0:1, :] = best.astype(F32)
    o_ref[1:2, :] = lo.astype(F32)
    o_ref[2:3, :] = hi.astype(F32)
    o_ref[3:4, :] = jnp.where(swap, g2, g1)
    o_ref[4:5, :] = jnp.where(swap, g1, g2)
    o_ref[5:8, :] = jnp.zeros((3, best.shape[1]), F32)


def _router_specs(bm, bn):
    in_specs = [pl.BlockSpec((1, bn), lambda i, j: (0, j)),
                pl.BlockSpec((bn, HEAD_DIM), lambda i, j: (j, 0)),
                pl.BlockSpec((bn, HEAD_DIM), lambda i, j: (j, 0)),
                pl.BlockSpec((N_EXPERTS, 1), lambda i, j: (0, 0))]
    out_spec = pl.BlockSpec((8, bm), lambda i, j: (0, i))
    scratch = [pltpu.VMEM((bm, HEAD_DIM), F32), pltpu.VMEM((bm, 1), F32)]
    return in_specs, out_spec, scratch


def _moe_plan(route, n_tiles):
    t = route.shape[1]
    i32 = jnp.int32
    best, lo, hi = route[0].astype(i32), route[1].astype(i32), route[2].astype(i32)
    pair = lo * (7 - lo) // 2 + (hi - lo - 1)
    bucket = best * N_PAIRS + pair
    onehot = bucket[:, None] == jnp.arange(N_BUCKETS, dtype=i32)[None, :]
    csum = jnp.cumsum(onehot.astype(i32), axis=0)
    counts = csum[-1]
    tiles = (counts + MOE_TILE - 1) // MOE_TILE
    tile_end = jnp.cumsum(tiles)
    tile_start = tile_end - tiles
    total = tile_end[-1]
    pos = jnp.sum(jnp.where(onehot, csum - 1 + tile_start[None, :] * MOE_TILE, 0), axis=1)
    packed = jnp.stack([jnp.arange(t, dtype=F32), route[3], route[4]], axis=1)
    spare = (t + jnp.arange(n_tiles * MOE_TILE, dtype=i32) % MOE_TILE).astype(F32)
    init = jnp.stack([spare, jnp.zeros_like(spare), jnp.zeros_like(spare)], axis=1)
    rows = init.at[pos].set(packed, unique_indices=True)
    token_of = rows[:, 0].astype(i32)
    gates = rows[:, 1:3]
    tile_ids = jnp.arange(n_tiles, dtype=i32)
    tile_valid = (tile_ids < total).astype(i32)
    tile_bucket = jnp.sum(tile_end[None, :] <= jnp.minimum(tile_ids, total - 1)[:, None], axis=1).astype(i32)
    tile_group = tile_bucket // N_PAIRS
    tile_pair = tile_bucket % N_PAIRS
    pair_lo = (tile_pair >= 3).astype(i32) + (tile_pair >= 5).astype(i32)
    pair_hi = tile_pair - pair_lo * (7 - pair_lo) // 2 + pair_lo + 1
    odd = (tile_ids & 1) == 1
    first = tile_group * EXPERTS_PER_GROUP + jnp.where(odd, pair_hi, pair_lo)
    second = tile_group * EXPERTS_PER_GROUP + jnp.where(odd, pair_lo, pair_hi)
    step_expert = jnp.stack([first, second], axis=1).reshape(-1).astype(i32)
    last = jnp.sum(jnp.where(jnp.arange(2 * n_tiles, dtype=i32) == 2 * total - 1, step_expert, 0))
    step_expert = jnp.where(jnp.repeat(tile_valid, 2) > 0, step_expert, last)
    return step_expert, tile_valid, token_of, gates


def _moe_kernel(se_ref, tv_ref, tok_ref, x_hbm, gates_ref, gain_ref, wg_ref, wu_ref, wd_ref,
                out_hbm, xbuf, x2_scr, hbuf, obuf, sem_in, sem_out):
    del se_ref
    n_tiles = pl.num_programs(0)
    n_tokens = x_hbm.shape[0]
    i = pl.program_id(0)
    s = pl.program_id(1)
    valid = tv_ref[i] > 0
    next_valid = tv_ref[jnp.minimum(i + 1, n_tiles - 1)] * (i + 1 < n_tiles) > 0

    def gather_start(tile):
        def body(r, carry):
            tok = tok_ref[tile * MOE_TILE + r]
            src = jnp.where(tok < n_tokens, tok, 0)
            pltpu.make_async_copy(x_hbm.at[pl.ds(src, 1)], xbuf.at[pl.ds(r, 1)], sem_in).start()
            return carry
        lax.fori_loop(0, MOE_TILE, body, 0, unroll=8)

    def gather_wait():
        pltpu.make_async_copy(x_hbm.at[pl.ds(0, MOE_TILE)], xbuf, sem_in).wait()

    def scatter_start(tile):
        def body(r, carry):
            tok = tok_ref[tile * MOE_TILE + r]
            pltpu.make_async_copy(obuf.at[pl.ds(r, 1)], out_hbm.at[pl.ds(tok, 1)], sem_out).start()
            return carry
        lax.fori_loop(0, MOE_TILE, body, 0, unroll=8)

    def scatter_wait():
        pltpu.make_async_copy(obuf, out_hbm.at[pl.ds(0, MOE_TILE)], sem_out).wait()

    @pl.when(valid & (s == 0))
    def _():
        @pl.when(i == 0)
        def _():
            gather_start(0)
            obuf[...] = jnp.zeros(obuf.shape, obuf.dtype)
            spare = pltpu.make_async_copy(obuf, out_hbm.at[pl.ds(n_tokens, MOE_TILE)], sem_out)
            spare.start()
            spare.wait()

        gather_wait()
        x = _rows_2d(xbuf, x2_scr)
        hbuf[...] = _rms(x, gain_ref[...]).astype(BF16)

        @pl.when(next_valid)
        def _():
            gather_start(i + 1)

    @pl.when(valid)
    def _():
        which = s ^ (i & 1)
        gate = jnp.where(which == 0, gates_ref[:, 0:1], gates_ref[:, 1:2])
        h = hbuf[...]
        hg = jnp.dot(h, wg_ref[...], preferred_element_type=F32)
        hu = jnp.dot(h, wu_ref[...], preferred_element_type=F32)
        act = (jax.nn.silu(hg) * hu * gate).astype(BF16)
        y = jnp.dot(act, wd_ref[...], preferred_element_type=F32)

        @pl.when(s == 0)
        def _():
            x2_scr[...] = x2_scr[...] + y

        @pl.when(s == 1)
        def _():
            @pl.when(i > 0)
            def _():
                scatter_wait()

            obuf[...] = (x2_scr[...] + y).reshape(obuf.shape)
            scatter_start(i)

            @pl.when(jnp.logical_not(next_valid))
            def _():
                scatter_wait()


def _grouped_moe(x3, route, gain, wg_bf16, wu_bf16, wd_bf16, li):
    t, _, d = x3.shape
    f = wg_bf16.shape[3]
    n_tiles = t // MOE_TILE + N_BUCKETS
    step_expert, tile_valid, token_of, gates = _moe_plan(route, n_tiles)
    grid_spec = pltpu.PrefetchScalarGridSpec(
        num_scalar_prefetch=3,
        grid=(n_tiles, 2),
        in_specs=[pl.BlockSpec(memory_space=pl.ANY),
                  pl.BlockSpec((MOE_TILE, 2), lambda i, s, se, tv, tok: (i, 0)),
                  pl.BlockSpec((1, d), lambda i, s, se, tv, tok: (0, 0)),
                  pl.BlockSpec((None, None, d, f), lambda i, s, se, tv, tok: (li, se[2 * i + s], 0, 0)),
                  pl.BlockSpec((None, None, d, f), lambda i, s, se, tv, tok: (li, se[2 * i + s], 0, 0)),
                  pl.BlockSpec((None, None, f, d), lambda i, s, se, tv, tok: (li, se[2 * i + s], 0, 0))],
        out_specs=pl.BlockSpec(memory_space=pl.ANY),
        scratch_shapes=[pltpu.VMEM((MOE_TILE, 1, d), F32),
                        pltpu.VMEM((MOE_TILE, d), F32),
                        pltpu.VMEM((MOE_TILE, d), BF16),
                        pltpu.VMEM((MOE_TILE, 1, d), F32),
                        pltpu.SemaphoreType.DMA(()),
                        pltpu.SemaphoreType.DMA(())],
    )
    return pl.pallas_call(
        _moe_kernel,
        grid_spec=grid_spec,
        out_shape=jax.ShapeDtypeStruct((t + MOE_TILE, 1, d), F32),
        compiler_params=_cparams("arbitrary", "arbitrary"),
        name="moe_experts",
    )(step_expert, tile_valid, token_of, x3, gates, gain[None, :], wg_bf16, wu_bf16, wd_bf16)


def kernel(x, positions, norm_mix, norm_ffn, a_w_in, a_q_norm, a_k_norm, b_w_group, b_scale, ab_w_out,
           c_w_in, c_v_norm, c_w_spatial, c_b_spatial, c_w_out, router_w, router_bias,
           expert_w_gate, expert_w_up, expert_w_down):
    batch, t, d = x.shape
    assert batch == 1 and t % (N_STREAM * ATTN_BLOCK) == 0
    depth = norm_mix.shape[0]
    a_width = (a_w_in.shape[2] - b_w_group.shape[1] * b_w_group.shape[2]) // 3
    n_heads = a_width // HEAD_DIM
    ns = t // N_STREAM

    pos_rows = positions[0].reshape(ns, N_STREAM).T.reshape(t)
    cos, sin = (tab.reshape(N_STREAM, ns, HEAD_DIM) for tab in _trig_tables(pos_rows))
    router_w_pad = jnp.pad(router_w, ((0, 0), (0, HEAD_DIM - N_EXPERTS)))
    router_hi = router_w_pad.astype(BF16)
    router_lo = (router_w_pad - router_hi.astype(F32)).astype(BF16)
    a_w_in, b_w_group, ab_w_out, c_w_in, c_w_out, expert_w_gate, expert_w_up, expert_w_down = (
        w.astype(BF16) for w in (a_w_in, b_w_group, ab_w_out, c_w_in, c_w_out,
                                 expert_w_gate, expert_w_up, expert_w_down))

    xs = x[0]
    for layer in range(depth):
        i = layer // 2
        router = (norm_ffn[layer][None, :], router_hi, router_lo, router_bias[:, None])
        if layer % 2 == 0:
            proj3 = _even_in_proj(xs, t, norm_mix[layer], a_w_in, i,
                                  jnp.stack([a_q_norm[i], a_k_norm[i]]), cos, sin, a_width)
            oa3 = _dilated_attention(proj3, n_heads)
            ob3 = _pooling_mixer(proj3, b_w_group, i, b_scale[i])
            x3, route = _even_out_proj(oa3, ob3, ab_w_out, i, xs, t, router)
        else:
            z = _odd_in_proj(xs, t, norm_mix[layer], c_w_in, i)
            x3, route = _odd_out_proj(z, c_v_norm[i], c_w_spatial[i], c_b_spatial[i], c_w_out, i, xs, t, router)
        xs = _grouped_moe(x3, route, norm_ffn[layer], expert_w_gate, expert_w_up, expert_w_down, layer)
    return xs[:t].reshape(1, t, d)
```

```python
import functools

import jax
import jax.numpy as jnp
from jax import lax
from jax.experimental import pallas as pl
from jax.experimental.pallas import tpu as pltpu

F32 = jnp.float32
BF16 = jnp.bfloat16

EPS = 1e-6
HEAD_DIM = 128
ROPE_DIM = HEAD_DIM // 4
ROPE_HALF = ROPE_DIM // 2
ROPE_THETA = 500000.0
POOL_SIZES = (2, 4, 8, 16)
N_STREAM = 16
ATTN_BLOCK = 128
DILATIONS = (16, 4, 1)
GM_GROUPS = 8
GM_CHUNK = 128
N_EXPERTS = 16
N_EXPERT_GROUPS = 4
EXPERTS_PER_GROUP = 4
N_PAIRS = 6
N_BUCKETS = N_EXPERT_GROUPS * N_PAIRS
MOE_TILE = 256
V7X_VMEM_LIMIT = 56 * 1024 * 1024


def _cparams(*sem):
    return pltpu.CompilerParams(dimension_semantics=sem, vmem_limit_bytes=V7X_VMEM_LIMIT)


def _rms(xf, gain_row):
    ms = jnp.mean(xf * xf, axis=-1, keepdims=True)
    return xf * lax.rsqrt(ms + EPS) * gain_row


def _rows_2d(x_ref, x2_scr):
    if len(x_ref.shape) == 2:
        return x_ref[...]
    x2_scr[...] = x_ref[...].reshape(x2_scr.shape)
    return x2_scr[...]


def _x_spec(x, bm, bn, index):
    if x.ndim == 2:
        return pl.BlockSpec((bm, bn), lambda *g: index(*g))
    return pl.BlockSpec((bm, 1, bn), lambda *g: (index(*g)[0], 0, index(*g)[1]))


def _trig_kernel(pos_ref, freq_ref, cos_ref, sin_ref):
    ang = pos_ref[...].astype(F32) * freq_ref[...]
    lane = lax.broadcasted_iota(jnp.int32, ang.shape, 1)
    c = jnp.cos(ang)
    s = jnp.sin(ang)
    cos_ref[...] = jnp.where(lane < ROPE_DIM, c, 1.0)
    sin_ref[...] = jnp.where(lane < ROPE_HALF, -s, jnp.where(lane < ROPE_DIM, s, 0.0))


def _trig_tables(pos_rows):
    t = pos_rows.shape[0]
    inv_freq = ROPE_THETA ** (-jnp.arange(ROPE_HALF, dtype=F32) / ROPE_HALF)
    freq = jnp.tile(inv_freq, HEAD_DIM // ROPE_HALF)[None, :]
    pos_b = jnp.broadcast_to(pos_rows[:, None], (t, HEAD_DIM))
    bm = 1024
    return pl.pallas_call(
        _trig_kernel,
        grid=(t // bm,),
        in_specs=[pl.BlockSpec((bm, HEAD_DIM), lambda i: (i, 0)),
                  pl.BlockSpec((1, HEAD_DIM), lambda i: (0, 0))],
        out_specs=[pl.BlockSpec((bm, HEAD_DIM), lambda i: (i, 0))] * 2,
        out_shape=[jax.ShapeDtypeStruct((t, HEAD_DIM), F32)] * 2,
        compiler_params=_cparams("arbitrary"),
        name="rope_tables",
    )(pos_b, freq)


def _even_in_kernel(x_ref, gain_ref, w_ref, qkg_ref, cos_ref, sin_ref, o_ref, x2_scr, h_scr, acc_scr, *, n_qk_blocks):
    j = pl.program_id(1)
    n_col = pl.num_programs(1) - 1

    bs = o_ref.shape[1]
    bm = N_STREAM * bs

    def matmul():
        acc_scr[j % 2] = jnp.dot(h_scr[...], w_ref[...], preferred_element_type=F32)

    def finish_qk():
        prev = acc_scr.at[(j - 1) % 2]
        gain = jnp.where(j - 1 < n_qk_blocks // 2, qkg_ref[0:1, :], qkg_ref[1:2, :])
        cos = cos_ref[...].reshape(bm, HEAD_DIM)
        sin = sin_ref[...].reshape(bm, HEAD_DIM)
        lane = lax.broadcasted_iota(jnp.int32, cos.shape, 1)
        for hh in range(prev.shape[1] // HEAD_DIM):
            y = _rms(prev[:, hh * HEAD_DIM:(hh + 1) * HEAD_DIM], gain)
            swapped = jnp.where(lane < ROPE_HALF,
                                pltpu.roll(y, HEAD_DIM - ROPE_HALF, 1),
                                pltpu.roll(y, ROPE_HALF, 1))
            o_ref[:, :, hh * HEAD_DIM:(hh + 1) * HEAD_DIM] = (y * cos + swapped * sin).reshape(N_STREAM, bs, HEAD_DIM)

    def finish_plain():
        o_ref[...] = acc_scr[(j - 1) % 2].reshape(o_ref.shape)

    @pl.when(j == 0)
    def _():
        i = lax.broadcasted_iota(jnp.int32, (bm, bm), 0)
        c = lax.broadcasted_iota(jnp.int32, (bm, bm), 1)
        perm = jnp.where(c == (i % bs) * N_STREAM + i // bs, 1.0, 0.0).astype(BF16)
        h = _rms(_rows_2d(x_ref, x2_scr), gain_ref[...]).astype(BF16)
        h_scr[...] = jnp.dot(perm, h, preferred_element_type=F32).astype(BF16)
        matmul()

    @pl.when((j >= 1) & (j <= n_qk_blocks))
    def _():
        finish_qk()
        matmul()

    @pl.when((j > n_qk_blocks) & (j < n_col))
    def _():
        finish_plain()
        matmul()

    @pl.when(j == n_col)
    def _():
        finish_plain()


def _even_in_proj(x, t, gain, w_bf16, li, qk_gain, cos3, sin3, a_width):
    d = x.shape[-1]
    n = w_bf16.shape[2]
    ns = t // N_STREAM
    bm, bn = 512, 512
    bs = bm // N_STREAM
    n_col = n // bn
    kern = functools.partial(_even_in_kernel, n_qk_blocks=2 * a_width // bn)
    return pl.pallas_call(
        kern,
        grid=(t // bm, n_col + 1),
        in_specs=[_x_spec(x, bm, d, lambda i, j: (i, 0)),
                  pl.BlockSpec((1, d), lambda i, j: (0, 0)),
                  pl.BlockSpec((None, d, bn), lambda i, j: (li, 0, jnp.minimum(j, n_col - 1))),
                  pl.BlockSpec((2, HEAD_DIM), lambda i, j: (0, 0)),
                  pl.BlockSpec((N_STREAM, bs, HEAD_DIM), lambda i, j: (0, i, 0)),
                  pl.BlockSpec((N_STREAM, bs, HEAD_DIM), lambda i, j: (0, i, 0))],
        out_specs=pl.BlockSpec((N_STREAM, bs, bn), lambda i, j: (0, i, jnp.maximum(j - 1, 0))),
        out_shape=jax.ShapeDtypeStruct((N_STREAM, ns, n), F32),
        scratch_shapes=[pltpu.VMEM((bm, d), F32), pltpu.VMEM((bm, d), BF16), pltpu.VMEM((2, bm, bn), F32)],
        compiler_params=_cparams("arbitrary", "arbitrary"),
        name="even_in_proj",
    )(x, gain[None, :], w_bf16, qk_gain, cos3, sin3)


def _attn_kernel(q_ref, k_ref, v_ref, o_ref, o_scr, l_scr):
    sb = pl.program_id(1)
    scale = HEAD_DIM ** -0.5
    base = pl.multiple_of(sb * ATTN_BLOCK, ATTN_BLOCK)
    prev_base = pl.multiple_of(jnp.maximum(base - ATTN_BLOCK, 0), ATTN_BLOCK)
    row = lax.broadcasted_iota(jnp.int32, (ATTN_BLOCK, 2 * ATTN_BLOCK), 0)
    col = lax.broadcasted_iota(jnp.int32, (ATTN_BLOCK, 2 * ATTN_BLOCK), 1)
    is_prev = col < ATTN_BLOCK
    colk = col & (ATTN_BLOCK - 1)

    for pat, d in enumerate(DILATIONS):
        c = N_STREAM // d
        cl = ATTN_BLOCK // c
        sh = cl.bit_length() - 1
        qpos = c * (row & (cl - 1)) + (row >> sh)
        kpos = c * (colk & (cl - 1)) + (colk >> sh)
        bias = jnp.where(is_prev,
                         jnp.where(qpos <= kpos, 0.0, -jnp.inf),
                         jnp.where(qpos >= kpos, 0.0, -jnp.inf)).astype(F32)
        bias_start = jnp.where(jnp.logical_and(is_prev, sb == 0), -jnp.inf, bias)

        for r_d in range(d):
            for jj in range(c):
                lo = jj * cl
                rows = [r_d + d * a for a in range(c)]
                if jj == 0:
                    prev = pl.ds(prev_base + (ATTN_BLOCK - cl), cl)
                else:
                    prev = pl.ds(base + (lo - cl), cl)
                cur = pl.ds(base + lo, cl)
                q = jnp.concatenate([q_ref[r, lo:lo + cl, :] for r in rows], axis=0).astype(BF16)
                k = jnp.concatenate([k_ref[r, prev, :] for r in rows] + [k_ref[r, cur, :] for r in rows],
                                    axis=0).astype(BF16)
                v = jnp.concatenate([v_ref[r, prev, :] for r in rows] + [v_ref[r, cur, :] for r in rows],
                                    axis=0).astype(BF16)
                s = lax.dot_general(q, k, (((1,), (1,)), ((), ())), preferred_element_type=F32) * scale
                s = s + (bias_start if jj == 0 else bias)
                m = jnp.max(s, axis=-1, keepdims=True)
                p = jnp.exp(s - m)
                den = jnp.sum(p, axis=-1, keepdims=True)
                o = jnp.dot(p.astype(BF16), v, preferred_element_type=F32) / den
                lse = jnp.broadcast_to(m + jnp.log(den), (ATTN_BLOCK, HEAD_DIM))
                for a, r in enumerate(rows):
                    o_scr[pat, r, lo:lo + cl, :] = o[a * cl:(a + 1) * cl, :]
                    l_scr[pat, r, lo:lo + cl, :] = lse[a * cl:(a + 1) * cl, :]

    def mix(r, carry):
        l0, l1, l2 = l_scr[0, r], l_scr[1, r], l_scr[2, r]
        mx = jnp.maximum(jnp.maximum(l0, l1), l2)
        w0, w1, w2 = jnp.exp(l0 - mx), jnp.exp(l1 - mx), jnp.exp(l2 - mx)
        num = w0 * o_scr[0, r] + w1 * o_scr[1, r] + w2 * o_scr[2, r]
        o_ref[r] = (num / (w0 + w1 + w2)).astype(o_ref.dtype)
        return carry

    lax.fori_loop(0, N_STREAM, mix, 0)


def _dilated_attention(proj3, n_heads):
    _, ns, n = proj3.shape
    n_sb = ns // ATTN_BLOCK
    out3 = pl.pallas_call(
        _attn_kernel,
        grid=(n_heads, n_sb),
        in_specs=[pl.BlockSpec((N_STREAM, ATTN_BLOCK, HEAD_DIM), lambda h, sb: (0, sb, h)),
                  pl.BlockSpec((N_STREAM, ns, HEAD_DIM), lambda h, sb: (0, 0, n_heads + h)),
                  pl.BlockSpec((N_STREAM, ns, HEAD_DIM), lambda h, sb: (0, 0, 2 * n_heads + h))],
        out_specs=pl.BlockSpec((N_STREAM, ATTN_BLOCK, HEAD_DIM), lambda h, sb: (0, sb, h)),
        out_shape=jax.ShapeDtypeStruct((N_STREAM, ns, n_heads * HEAD_DIM), BF16),
        scratch_shapes=[pltpu.VMEM((3, N_STREAM, ATTN_BLOCK, HEAD_DIM), F32),
                        pltpu.VMEM((3, N_STREAM, ATTN_BLOCK, HEAD_DIM), F32)],
        compiler_params=_cparams("arbitrary", "arbitrary"),
        name="dilated_attention",
    )(proj3, proj3, proj3)
    return out3


def _pool_kernel(pb_ref, w_ref, sc_ref, o_ref, pre_scr):
    g = pl.program_id(0)
    ns, gw = pb_ref.shape[1], pb_ref.shape[2]
    first = lax.broadcasted_iota(jnp.int32, (ns, gw), 0) == 0

    for r in range(N_STREAM):
        pre_scr[r] = pb_ref[r] if r == 0 else pre_scr[r - 1] + pb_ref[r]

    for gi, p in enumerate(POOL_SIZES):
        @pl.when(g == gi)
        def _(p=p):
            for r in range(N_STREAM):
                win = pre_scr[r] - pre_scr[r - p] if r - p >= 0 else pre_scr[r]
                if r - p + 1 < 0:
                    wrap = pre_scr[N_STREAM - 1] - pre_scr[r - p + N_STREAM]
                    win = win + jnp.where(first, 0.0, pltpu.roll(wrap, 1, 0))
                cnt = jnp.where(first, float(min(r + 1, p)), float(p))
                pooled = win / cnt - pb_ref[r]
                y = jnp.dot(pooled.astype(BF16), w_ref[...], preferred_element_type=F32) * sc_ref[...]
                o_ref[r] = y.astype(o_ref.dtype)


def _pooling_mixer(proj3, w_group_bf16, li, scale):
    _, ns, n = proj3.shape
    _, n_groups, gw, _ = w_group_bf16.shape
    pb_blk0 = (n - n_groups * gw) // gw
    return pl.pallas_call(
        _pool_kernel,
        grid=(n_groups,),
        in_specs=[pl.BlockSpec((N_STREAM, ns, gw), lambda g: (0, 0, pb_blk0 + g)),
                  pl.BlockSpec((None, None, gw, gw), lambda g: (li, g, 0, 0)),
                  pl.BlockSpec((1, gw), lambda g: (0, g))],
        out_specs=pl.BlockSpec((N_STREAM, ns, gw), lambda g: (0, 0, g)),
        out_shape=jax.ShapeDtypeStruct((N_STREAM, ns, n_groups * gw), BF16),
        scratch_shapes=[pltpu.VMEM((N_STREAM, ns, gw), F32)],
        compiler_params=_cparams("arbitrary"),
        name="pooling_mixer",
    )(proj3, w_group_bf16, scale[None, :])


def _even_out_kernel(a_ref, b_ref, wa_ref, wb_ref, x_ref, fg_ref, rhi_ref, rlo_ref, rb_ref,
                     o_ref, route_ref, a_scr, b_scr, xn_scr, lg_scr, ss_scr, *, width):
    j = pl.program_id(1)
    n_col = pl.num_programs(1) - 1
    bs = a_ref.shape[1]
    bm = N_STREAM * bs

    def project():
        slot = xn_scr.at[j % 2]
        mix = (jnp.dot(a_scr[...], wa_ref[...], preferred_element_type=F32)
               + jnp.dot(b_scr[...], wb_ref[...], preferred_element_type=F32))
        xn = _rows_2d(x_ref, slot) + mix
        slot[...] = xn
        o_ref[...] = xn.reshape(o_ref.shape)

    def route_prev(last):
        _router_accumulate(xn_scr[(j - 1) % 2], last, fg_ref, rhi_ref, rlo_ref, rb_ref, route_ref,
                           lg_scr, ss_scr, width)

    @pl.when(j == 0)
    def _():
        n = lax.broadcasted_iota(jnp.int32, (bm, bm), 0)
        c = lax.broadcasted_iota(jnp.int32, (bm, bm), 1)
        perm = jnp.where(c == (n % N_STREAM) * bs + n // N_STREAM, 1.0, 0.0).astype(BF16)
        a = a_ref[...].reshape(bm, a_ref.shape[2])
        b = b_ref[...].reshape(bm, b_ref.shape[2])
        a_scr[...] = jnp.dot(perm, a, preferred_element_type=F32).astype(BF16)
        b_scr[...] = jnp.dot(perm, b, preferred_element_type=F32).astype(BF16)
        lg_scr[...] = jnp.zeros(lg_scr.shape, F32)
        ss_scr[...] = jnp.zeros(ss_scr.shape, F32)
        project()

    @pl.when((j >= 1) & (j < n_col))
    def _():
        route_prev(False)
        project()

    @pl.when(j == n_col)
    def _():
        route_prev(True)


def _even_out_proj(oa3, ob3, w_bf16, li, x, t, router):
    d = x.shape[-1]
    ka, kb = oa3.shape[2], ob3.shape[2]
    assert ka % kb == 0
    bm, bn = 512, 1024
    bs = bm // N_STREAM
    n_col = d // bn
    cur = lambda j: jnp.minimum(j, n_col - 1)
    r_in, r_out, r_scr = _router_specs(bm, bn)
    return pl.pallas_call(
        functools.partial(_even_out_kernel, width=d),
        grid=(t // bm, n_col + 1),
        in_specs=[pl.BlockSpec((N_STREAM, bs, ka), lambda i, j: (0, i, 0)),
                  pl.BlockSpec((N_STREAM, bs, kb), lambda i, j: (0, i, 0)),
                  pl.BlockSpec((None, ka, bn), lambda i, j: (li, 0, cur(j))),
                  pl.BlockSpec((None, kb, bn), lambda i, j: (li, ka // kb, cur(j))),
                  _x_spec(x, bm, bn, lambda i, j: (i, cur(j)))] + r_in,
        out_specs=[pl.BlockSpec((bm, 1, bn), lambda i, j: (i, 0, cur(j))), r_out],
        out_shape=[jax.ShapeDtypeStruct((t, 1, d), F32), jax.ShapeDtypeStruct((8, t), F32)],
        scratch_shapes=[pltpu.VMEM((bm, ka), BF16), pltpu.VMEM((bm, kb), BF16), pltpu.VMEM((2, bm, bn), F32)] + r_scr,
        compiler_params=_cparams("arbitrary", "arbitrary"),
        name="even_out_proj",
    )(oa3, ob3, w_bf16, w_bf16, x, *router)


def _odd_in_kernel(x_ref, gain_ref, w_ref, o_ref, x2_scr, h_scr):
    @pl.when(pl.program_id(1) == 0)
    def _():
        h_scr[...] = _rms(_rows_2d(x_ref, x2_scr), gain_ref[...]).astype(BF16)

    acc = jnp.dot(h_scr[...], w_ref[...], preferred_element_type=F32)
    o_ref[...] = jax.nn.gelu(acc).astype(o_ref.dtype)


def _odd_in_proj(x, t, gain, w_bf16, li):
    d = x.shape[-1]
    n = w_bf16.shape[2]
    bm, bn = 512, 1024
    return pl.pallas_call(
        _odd_in_kernel,
        grid=(t // bm, n // bn),
        in_specs=[_x_spec(x, bm, d, lambda i, j: (i, 0)),
                  pl.BlockSpec((1, d), lambda i, j: (0, 0)),
                  pl.BlockSpec((None, d, bn), lambda i, j: (li, 0, j))],
        out_specs=pl.BlockSpec((bm, bn), lambda i, j: (i, j)),
        out_shape=jax.ShapeDtypeStruct((t, n), BF16),
        scratch_shapes=[pltpu.VMEM((bm, d), F32), pltpu.VMEM((bm, d), BF16)],
        compiler_params=_cparams("arbitrary", "arbitrary"),
        name="odd_in_proj",
    )(x, gain[None, :], w_bf16)


def _odd_out_kernel(u_ref, v_ref, vg_ref, ws_ref, bcol_ref, w_ref, x_ref, fg_ref, rhi_ref, rlo_ref, rb_ref,
                    o_ref, route_ref, g_scr, xn_scr, lg_scr, ss_scr):
    j = pl.program_id(1)
    n_col = pl.num_programs(1) - 1

    def project():
        slot = xn_scr.at[j % 2]
        xn = _rows_2d(x_ref, slot) + jnp.dot(g_scr[...], w_ref[...], preferred_element_type=F32)
        slot[...] = xn
        o_ref[...] = xn.reshape(o_ref.shape)

    def route_prev(last):
        _router_accumulate(xn_scr[(j - 1) % 2], last, fg_ref, rhi_ref, rlo_ref, rb_ref, route_ref,
                           lg_scr, ss_scr, g_scr.shape[1])

    @pl.when(j == 0)
    def _():
        bm, width = g_scr.shape
        gd = width // GM_GROUPS
        ssq = jnp.zeros((bm, 1), F32)
        for g in range(GM_GROUPS):
            vg = v_ref[:, g * gd:(g + 1) * gd].astype(F32)
            ssq = ssq + jnp.sum(vg * vg, axis=-1, keepdims=True)
        inv = lax.rsqrt(ssq / width + EPS)
        causal = (lax.broadcasted_iota(jnp.int32, (GM_CHUNK, GM_CHUNK), 0)
                  >= lax.broadcasted_iota(jnp.int32, (GM_CHUNK, GM_CHUNK), 1))
        wsum = jnp.where(causal, ws_ref[0], 0.0)
        for g in range(1, GM_GROUPS):
            wsum = wsum + jnp.where(causal, ws_ref[g], 0.0)
        wsum = wsum.astype(BF16)
        for g in range(GM_GROUPS):
            bcol = bcol_ref[:, g:g + 1]
            cols = slice(g * gd, (g + 1) * gd)
            for cc in range(bm // GM_CHUNK):
                rows = slice(cc * GM_CHUNK, (cc + 1) * GM_CHUNK)
                vn = (v_ref[rows, cols].astype(F32) * inv[rows, :] * vg_ref[:, cols]).astype(BF16)
                sv = jnp.dot(wsum, vn, preferred_element_type=F32) + bcol
                g_scr[rows, cols] = (u_ref[rows, cols].astype(F32) * sv).astype(BF16)
        lg_scr[...] = jnp.zeros(lg_scr.shape, F32)
        ss_scr[...] = jnp.zeros(ss_scr.shape, F32)
        project()

    @pl.when((j >= 1) & (j < n_col))
    def _():
        route_prev(False)
        project()

    @pl.when(j == n_col)
    def _():
        route_prev(True)


def _odd_out_proj(z, v_gain, w_spatial, b_spatial, w_bf16, li, x, t, router):
    d = x.shape[-1]
    width = z.shape[1] // 2
    assert width == d
    bm, bn = 512, 1024
    n_col = d // bn
    cur = lambda j: jnp.minimum(j, n_col - 1)
    r_in, r_out, r_scr = _router_specs(bm, bn)
    return pl.pallas_call(
        _odd_out_kernel,
        grid=(t // bm, n_col + 1),
        in_specs=[pl.BlockSpec((bm, width), lambda i, j: (i, 0)),
                  pl.BlockSpec((bm, width), lambda i, j: (i, 1)),
                  pl.BlockSpec((1, width), lambda i, j: (0, 0)),
                  pl.BlockSpec((GM_GROUPS, GM_CHUNK, GM_CHUNK), lambda i, j: (0, 0, 0)),
                  pl.BlockSpec((GM_CHUNK, GM_GROUPS), lambda i, j: (0, 0)),
                  pl.BlockSpec((None, width, bn), lambda i, j: (li, 0, cur(j))),
                  _x_spec(x, bm, bn, lambda i, j: (i, cur(j)))] + r_in,
        out_specs=[pl.BlockSpec((bm, 1, bn), lambda i, j: (i, 0, cur(j))), r_out],
        out_shape=[jax.ShapeDtypeStruct((t, 1, d), F32), jax.ShapeDtypeStruct((8, t), F32)],
        scratch_shapes=[pltpu.VMEM((bm, width), BF16), pltpu.VMEM((2, bm, bn), F32)] + r_scr,
        compiler_params=_cparams("arbitrary", "arbitrary"),
        name="odd_out_proj",
    )(z, z, v_gain[None, :], w_spatial, b_spatial.T, w_bf16, x, *router)


def _router_accumulate(xn, last, gain_ref, whi_ref, wlo_ref, bias_ref, route_ref, lg_scr, ss_scr, width):
    xg = xn * gain_ref[...]
    hi = xg.astype(BF16)
    lo = (xg - hi.astype(F32)).astype(BF16)
    part = (jnp.dot(hi, whi_ref[...], preferred_element_type=F32)
            + jnp.dot(lo, whi_ref[...], preferred_element_type=F32)
            + jnp.dot(hi, wlo_ref[...], preferred_element_type=F32))
    lg_scr[...] = lg_scr[...] + part
    ss_scr[...] = ss_scr[...] + jnp.sum(xn * xn, axis=-1, keepdims=True)

    if last:
        _route_rows(lg_scr[...] * lax.rsqrt(ss_scr[...] / width + EPS), bias_ref, route_ref)


def _route_rows(logits, bias_ref, o_ref):
    lt = logits.T[0:N_EXPERTS, :]
    e = jnp.exp(lt - jnp.max(lt, axis=0, keepdims=True))
    scores = e / jnp.sum(e, axis=0, keepdims=True)
    biased = scores + bias_ref[...]
    sc = [scores[i:i + 1, :] for i in range(N_EXPERTS)]
    bi = [biased[i:i + 1, :] for i in range(N_EXPERTS)]

    def top2_sum(a, b, c, d):
        return jnp.maximum(jnp.maximum(jnp.maximum(a + b, a + c), jnp.maximum(a + d, b + c)),
                           jnp.maximum(b + d, c + d))

    grp = [top2_sum(*bi[EXPERTS_PER_GROUP * g:EXPERTS_PER_GROUP * (g + 1)]) for g in range(N_EXPERT_GROUPS)]
    best = jnp.zeros_like(grp[0], dtype=jnp.int32)
    best_score = grp[0]
    for g in range(1, N_EXPERT_GROUPS):
        upd = grp[g] > best_score
        best = jnp.where(upd, g, best)
        best_score = jnp.where(upd, grp[g], best_score)

    def pick(vals, k):
        out = vals[k]
        for g in range(1, N_EXPERT_GROUPS):
            out = jnp.where(best == g, vals[EXPERTS_PER_GROUP * g + k], out)
        return out

    vb = [pick(bi, k) for k in range(EXPERTS_PER_GROUP)]
    vs = [pick(sc, k) for k in range(EXPERTS_PER_GROUP)]
    i1 = jnp.zeros_like(best)
    m1 = vb[0]
    for k in range(1, EXPERTS_PER_GROUP):
        upd = vb[k] > m1
        i1 = jnp.where(upd, k, i1)
        m1 = jnp.where(upd, vb[k], m1)
    i2 = jnp.zeros_like(best)
    m2 = jnp.full_like(m1, -jnp.inf)
    for k in range(EXPERTS_PER_GROUP):
        upd = (i1 != k) & (vb[k] > m2)
        i2 = jnp.where(upd, k, i2)
        m2 = jnp.where(upd, vb[k], m2)

    def take(idx):
        out = vs[0]
        for k in range(1, EXPERTS_PER_GROUP):
            out = jnp.where(idx == k, vs[k], out)
        return out

    g1, g2 = take(i1), take(i2)
    tot = g1 + g2
    g1, g2 = g1 / tot, g2 / tot
    swap = i2 < i1
    lo = jnp.where(swap, i2, i1)
    hi = jnp.where(swap, i1, i2)
    o_ref[0:1, :] = best.astype(F32)
    o_ref[1:2, :] = lo.astype(F32)
    o_ref[2:3, :] = hi.astype(F32)
    o_ref[3:4, :] = jnp.where(swap, g2, g1)
    o_ref[4:5, :] = jnp.where(swap, g1, g2)
    o_ref[5:8, :] = jnp.zeros((3, best.shape[1]), F32)


def _router_specs(bm, bn):
    prev = lambda j: jnp.maximum(j - 1, 0)
    in_specs = [pl.BlockSpec((1, bn), lambda i, j: (0, prev(j))),
                pl.BlockSpec((bn, HEAD_DIM), lambda i, j: (prev(j), 0)),
                pl.BlockSpec((bn, HEAD_DIM), lambda i, j: (prev(j), 0)),
                pl.BlockSpec((N_EXPERTS, 1), lambda i, j: (0, 0))]
    out_spec = pl.BlockSpec((8, bm), lambda i, j: (0, i))
    scratch = [pltpu.VMEM((bm, HEAD_DIM), F32), pltpu.VMEM((bm, 1), F32)]
    return in_specs, out_spec, scratch


def _moe_plan(route, n_tiles):
    t = route.shape[1]
    i32 = jnp.int32
    best, lo, hi = route[0].astype(i32), route[1].astype(i32), route[2].astype(i32)
    pair = lo * (7 - lo) // 2 + (hi - lo - 1)
    bucket = best * N_PAIRS + pair
    onehot = bucket[:, None] == jnp.arange(N_BUCKETS, dtype=i32)[None, :]
    csum = jnp.cumsum(onehot.astype(i32), axis=0)
    counts = csum[-1]
    tiles = (counts + MOE_TILE - 1) // MOE_TILE
    tile_end = jnp.cumsum(tiles)
    tile_start = tile_end - tiles
    total = tile_end[-1]
    pos = jnp.sum(jnp.where(onehot, csum - 1 + tile_start[None, :] * MOE_TILE, 0), axis=1)
    packed = jnp.stack([jnp.arange(t, dtype=F32), route[3], route[4]], axis=1)
    spare = (t + jnp.arange(n_tiles * MOE_TILE, dtype=i32) % MOE_TILE).astype(F32)
    init = jnp.stack([spare, jnp.zeros_like(spare), jnp.zeros_like(spare)], axis=1)
    rows = init.at[pos].set(packed, unique_indices=True)
    token_of = rows[:, 0].astype(i32)
    gates = rows[:, 1:3]
    tile_ids = jnp.arange(n_tiles, dtype=i32)
    tile_valid = (tile_ids < total).astype(i32)
    tile_bucket = jnp.sum(tile_end[None, :] <= jnp.minimum(tile_ids, total - 1)[:, None], axis=1).astype(i32)
    tile_group = tile_bucket // N_PAIRS
    tile_pair = tile_bucket % N_PAIRS
    pair_lo = (tile_pair >= 3).astype(i32) + (tile_pair >= 5).astype(i32)
    pair_hi = tile_pair - pair_lo * (7 - pair_lo) // 2 + pair_lo + 1
    odd = (tile_ids & 1) == 1
    first = tile_group * EXPERTS_PER_GROUP + jnp.where(odd, pair_hi, pair_lo)
    second = tile_group * EXPERTS_PER_GROUP + jnp.where(odd, pair_lo, pair_hi)
    step_expert = jnp.stack([first, second], axis=1).reshape(-1).astype(i32)
    last = jnp.sum(jnp.where(jnp.arange(2 * n_tiles, dtype=i32) == 2 * total - 1, step_expert, 0))
    step_expert = jnp.where(jnp.repeat(tile_valid, 2) > 0, step_expert, last)
    return step_expert, tile_valid, token_of, gates


def _moe_kernel(se_ref, tv_ref, tok_ref, x_hbm, gates_ref, gain_ref, wg_ref, wu_ref, wd_ref,
                out_hbm, xbuf, x2_scr, hbuf, obuf, sem_in, sem_out):
    del se_ref
    n_tiles = pl.num_programs(0)
    n_tokens = x_hbm.shape[0]
    i = pl.program_id(0)
    s = pl.program_id(1)
    valid = tv_ref[i] > 0
    next_valid = tv_ref[jnp.minimum(i + 1, n_tiles - 1)] * (i + 1 < n_tiles) > 0

    def gather_start(tile):
        def body(r, carry):
            tok = tok_ref[tile * MOE_TILE + r]
            src = jnp.where(tok < n_tokens, tok, 0)
            pltpu.make_async_copy(x_hbm.at[pl.ds(src, 1)], xbuf.at[pl.ds(r, 1)], sem_in).start()
            return carry
        lax.fori_loop(0, MOE_TILE, body, 0, unroll=8)

    def gather_wait():
        pltpu.make_async_copy(x_hbm.at[pl.ds(0, MOE_TILE)], xbuf, sem_in).wait()

    def scatter_start(tile):
        def body(r, carry):
            tok = tok_ref[tile * MOE_TILE + r]
            pltpu.make_async_copy(obuf.at[pl.ds(r, 1)], out_hbm.at[pl.ds(tok, 1)], sem_out).start()
            return carry
        lax.fori_loop(0, MOE_TILE, body, 0, unroll=8)

    def scatter_wait():
        pltpu.make_async_copy(obuf, out_hbm.at[pl.ds(0, MOE_TILE)], sem_out).wait()

    def expert_mlp():
        which = s ^ (i & 1)
        gate = jnp.where(which == 0, gates_ref[:, 0:1], gates_ref[:, 1:2])
        h = hbuf[...]
        hg = jnp.dot(h, wg_ref[...], preferred_element_type=F32)
        hu = jnp.dot(h, wu_ref[...], preferred_element_type=F32)
        act = (jax.nn.silu(hg) * hu * gate).astype(BF16)
        return jnp.dot(act, wd_ref[...], preferred_element_type=F32)

    @pl.when(valid & (s == 0))
    def _():
        @pl.when(i == 0)
        def _():
            gather_start(0)
            obuf[...] = jnp.zeros(obuf.shape, obuf.dtype)

        gather_wait()
        x = _rows_2d(xbuf, x2_scr)
        hbuf[...] = _rms(x, gain_ref[...]).astype(BF16)

        nxt = jnp.minimum(i + 1, n_tiles - 1) * MOE_TILE
        prv = jnp.maximum(i - 1, 0) * MOE_TILE
        for r in range(MOE_TILE):
            tok = tok_ref[nxt + r]
            src = jnp.where(tok < n_tokens, tok, 0)
            pltpu.make_async_copy(x_hbm.at[pl.ds(src, 1)], xbuf.at[pl.ds(r, 1)], sem_in).start()
            dst = jnp.where(i > 0, tok_ref[prv + r], n_tokens + r)
            pltpu.make_async_copy(obuf.at[pl.ds(r, 1)], out_hbm.at[pl.ds(dst, 1)], sem_out).start()

        x2_scr[...] = x2_scr[...] + expert_mlp()

    @pl.when(valid & (s == 1))
    def _():
        @pl.when(jnp.logical_not(next_valid))
        def _():
            gather_wait()

        y = expert_mlp()
        scatter_wait()
        obuf[...] = (x2_scr[...] + y).reshape(obuf.shape)

        @pl.when(jnp.logical_not(next_valid))
        def _():
            scatter_start(i)
            scatter_wait()


def _grouped_moe(x3, route, gain, wg_bf16, wu_bf16, wd_bf16, li):
    t, _, d = x3.shape
    f = wg_bf16.shape[3]
    n_tiles = t // MOE_TILE + N_BUCKETS
    step_expert, tile_valid, token_of, gates = _moe_plan(route, n_tiles)
    grid_spec = pltpu.PrefetchScalarGridSpec(
        num_scalar_prefetch=3,
        grid=(n_tiles, 2),
        in_specs=[pl.BlockSpec(memory_space=pl.ANY),
                  pl.BlockSpec((MOE_TILE, 2), lambda i, s, se, tv, tok: (i, 0)),
                  pl.BlockSpec((1, d), lambda i, s, se, tv, tok: (0, 0)),
                  pl.BlockSpec((None, None, d, f), lambda i, s, se, tv, tok: (li, se[2 * i + s], 0, 0)),
                  pl.BlockSpec((None, None, d, f), lambda i, s, se, tv, tok: (li, se[2 * i + s], 0, 0)),
                  pl.BlockSpec((None, None, f, d), lambda i, s, se, tv, tok: (li, se[2 * i + s], 0, 0))],
        out_specs=pl.BlockSpec(memory_space=pl.ANY),
        scratch_shapes=[pltpu.VMEM((MOE_TILE, 1, d), F32),
                        pltpu.VMEM((MOE_TILE, d), F32),
                        pltpu.VMEM((MOE_TILE, d), BF16),
                        pltpu.VMEM((MOE_TILE, 1, d), F32),
                        pltpu.SemaphoreType.DMA(()),
                        pltpu.SemaphoreType.DMA(())],
    )
    return pl.pallas_call(
        _moe_kernel,
        grid_spec=grid_spec,
        out_shape=jax.ShapeDtypeStruct((t + MOE_TILE, 1, d), F32),
        compiler_params=_cparams("arbitrary", "arbitrary"),
        name="moe_experts",
    )(step_expert, tile_valid, token_of, x3, gates, gain[None, :], wg_bf16, wu_bf16, wd_bf16)


def kernel(x, positions, norm_mix, norm_ffn, a_w_in, a_q_norm, a_k_norm, b_w_group, b_scale, ab_w_out,
           c_w_in, c_v_norm, c_w_spatial, c_b_spatial, c_w_out, router_w, router_bias,
           expert_w_gate, expert_w_up, expert_w_down):
    batch, t, d = x.shape
    assert batch == 1 and t % (N_STREAM * ATTN_BLOCK) == 0
    depth = norm_mix.shape[0]
    a_width = (a_w_in.shape[2] - b_w_group.shape[1] * b_w_group.shape[2]) // 3
    n_heads = a_width // HEAD_DIM
    ns = t // N_STREAM

    pos_rows = positions[0].reshape(ns, N_STREAM).T.reshape(t)
    cos, sin = (tab.reshape(N_STREAM, ns, HEAD_DIM) for tab in _trig_tables(pos_rows))
    router_w_pad = jnp.pad(router_w, ((0, 0), (0, HEAD_DIM - N_EXPERTS)))
    router_hi = router_w_pad.astype(BF16)
    router_lo = (router_w_pad - router_hi.astype(F32)).astype(BF16)
    a_w_in, b_w_group, ab_w_out, c_w_in, c_w_out, expert_w_gate, expert_w_up, expert_w_down = (
        w.astype(BF16) for w in (a_w_in, b_w_group, ab_w_out, c_w_in, c_w_out,
                                 expert_w_gate, expert_w_up, expert_w_down))

    xs = x[0]
    for layer in range(depth):
        i = layer // 2
        router = (norm_ffn[layer][None, :], router_hi, router_lo, router_bias[:, None])
        if layer % 2 == 0:
            proj3 = _even_in_proj(xs, t, norm_mix[layer], a_w_in, i,
                                  jnp.stack([a_q_norm[i], a_k_norm[i]]), cos, sin, a_width)
            oa3 = _dilated_attention(proj3, n_heads)
            ob3 = _pooling_mixer(proj3, b_w_group, i, b_scale[i])
            x3, route = _even_out_proj(oa3, ob3, ab_w_out, i, xs, t, router)
        else:
            z = _odd_in_proj(xs, t, norm_mix[layer], c_w_in, i)
            x3, route = _odd_out_proj(z, c_v_norm[i], c_w_spatial[i], c_b_spatial[i], c_w_out, i, xs, t, router)
        xs = _grouped_moe(x3, route, norm_ffn[layer], expert_w_gate, expert_w_up, expert_w_down, layer)
    return xs[:t].reshape(1, t, d)
```

```python
import functools

import jax
import jax.numpy as jnp
from jax import lax
from jax.experimental import pallas as pl
from jax.experimental.pallas import tpu as pltpu

F32 = jnp.float32
BF16 = jnp.bfloat16

EPS = 1e-6
HEAD_DIM = 128
ROPE_DIM = HEAD_DIM // 4
ROPE_HALF = ROPE_DIM // 2
ROPE_THETA = 500000.0
POOL_SIZES = (2, 4, 8, 16)
N_STREAM = 16
ATTN_BLOCK = 128
DILATIONS = (16, 4, 1)
GM_GROUPS = 8
GM_CHUNK = 128
N_EXPERTS = 16
N_EXPERT_GROUPS = 4
EXPERTS_PER_GROUP = 4
N_PAIRS = 6
N_BUCKETS = N_EXPERT_GROUPS * N_PAIRS
MOE_TILE = 256
V7X_VMEM_LIMIT = 56 * 1024 * 1024


def _cparams(*sem):
    return pltpu.CompilerParams(dimension_semantics=sem, vmem_limit_bytes=V7X_VMEM_LIMIT)


def _rms(xf, gain_row):
    ms = jnp.mean(xf * xf, axis=-1, keepdims=True)
    return xf * lax.rsqrt(ms + EPS) * gain_row


def _rows_2d(x_ref, x2_scr):
    if len(x_ref.shape) == 2:
        return x_ref[...]
    x2_scr[...] = x_ref[...].reshape(x2_scr.shape)
    return x2_scr[...]


def _cast_specs(weights, layer, n_steps, step_of):
    n_blk = 1
    while n_blk * 2 <= min(n_steps, 128):
        n_blk *= 2
    views, in_specs, out_specs, out_shapes = [], [], [], []
    for w in weights:
        n_layers, e, a, b = w.shape
        rows = e * a // n_blk
        assert rows * n_blk == e * a and rows % 16 == 0
        blk = lambda *g: jnp.minimum(step_of(*g), n_blk - 1)
        views.append(w.reshape(n_layers, e * a, b))
        in_specs.append(pl.BlockSpec((None, rows, b), lambda *g, blk=blk: (layer, blk(*g), 0)))
        out_specs.append(pl.BlockSpec((rows, b), lambda *g, blk=blk: (blk(*g), 0)))
        out_shapes.append(jax.ShapeDtypeStruct((e * a, b), BF16))
    return views, in_specs, out_specs, out_shapes


def _cast_blocks(refs):
    n = len(refs) // 2
    for src, dst in zip(refs[:n], refs[n:]):
        dst[...] = src[...].astype(dst.dtype)


def _x_spec(x, bm, bn, index):
    if x.ndim == 2:
        return pl.BlockSpec((bm, bn), lambda *g: index(*g))
    return pl.BlockSpec((bm, 1, bn), lambda *g: (index(*g)[0], 0, index(*g)[1]))


def _trig_kernel(pos_ref, freq_ref, cos_ref, sin_ref):
    ang = pos_ref[...].astype(F32) * freq_ref[...]
    lane = lax.broadcasted_iota(jnp.int32, ang.shape, 1)
    c = jnp.cos(ang)
    s = jnp.sin(ang)
    cos_ref[...] = jnp.where(lane < ROPE_DIM, c, 1.0)
    sin_ref[...] = jnp.where(lane < ROPE_HALF, -s, jnp.where(lane < ROPE_DIM, s, 0.0))


def _trig_tables(pos_rows):
    t = pos_rows.shape[0]
    inv_freq = ROPE_THETA ** (-jnp.arange(ROPE_HALF, dtype=F32) / ROPE_HALF)
    freq = jnp.tile(inv_freq, HEAD_DIM // ROPE_HALF)[None, :]
    pos_b = jnp.broadcast_to(pos_rows[:, None], (t, HEAD_DIM))
    bm = 1024
    return pl.pallas_call(
        _trig_kernel,
        grid=(t // bm,),
        in_specs=[pl.BlockSpec((bm, HEAD_DIM), lambda i: (i, 0)),
                  pl.BlockSpec((1, HEAD_DIM), lambda i: (0, 0))],
        out_specs=[pl.BlockSpec((bm, HEAD_DIM), lambda i: (i, 0))] * 2,
        out_shape=[jax.ShapeDtypeStruct((t, HEAD_DIM), F32)] * 2,
        compiler_params=_cparams("arbitrary"),
        name="rope_tables",
    )(pos_b, freq)


def _even_in_kernel(x_ref, gain_ref, w_ref, qkg_ref, cos_ref, sin_ref, cg_ref, cu_ref, cd_ref,
                    o_ref, og_ref, ou_ref, od_ref, x2_scr, h_scr, acc_scr, *, n_qk_blocks):
    j = pl.program_id(1)
    n_col = pl.num_programs(1) - 1

    bs = o_ref.shape[1]
    bm = N_STREAM * bs

    def matmul():
        _cast_blocks((cg_ref, cu_ref, cd_ref, og_ref, ou_ref, od_ref))
        acc_scr[j % 2] = jnp.dot(h_scr[...], w_ref[...], preferred_element_type=F32)

    def finish_qk():
        prev = acc_scr.at[(j - 1) % 2]
        gain = jnp.where(j - 1 < n_qk_blocks // 2, qkg_ref[0:1, :], qkg_ref[1:2, :])
        cos = cos_ref[...].reshape(bm, HEAD_DIM)
        sin = sin_ref[...].reshape(bm, HEAD_DIM)
        lane = lax.broadcasted_iota(jnp.int32, cos.shape, 1)
        for hh in range(prev.shape[1] // HEAD_DIM):
            y = _rms(prev[:, hh * HEAD_DIM:(hh + 1) * HEAD_DIM], gain)
            swapped = jnp.where(lane < ROPE_HALF,
                                pltpu.roll(y, HEAD_DIM - ROPE_HALF, 1),
                                pltpu.roll(y, ROPE_HALF, 1))
            o_ref[:, :, hh * HEAD_DIM:(hh + 1) * HEAD_DIM] = (y * cos + swapped * sin).reshape(N_STREAM, bs, HEAD_DIM)

    def finish_plain():
        o_ref[...] = acc_scr[(j - 1) % 2].reshape(o_ref.shape)

    @pl.when(j == 0)
    def _():
        i = lax.broadcasted_iota(jnp.int32, (bm, bm), 0)
        c = lax.broadcasted_iota(jnp.int32, (bm, bm), 1)
        perm = jnp.where(c == (i % bs) * N_STREAM + i // bs, 1.0, 0.0).astype(BF16)
        h = _rms(_rows_2d(x_ref, x2_scr), gain_ref[...]).astype(BF16)
        h_scr[...] = jnp.dot(perm, h, preferred_element_type=F32).astype(BF16)
        matmul()

    @pl.when((j >= 1) & (j <= n_qk_blocks))
    def _():
        finish_qk()
        matmul()

    @pl.when((j > n_qk_blocks) & (j < n_col))
    def _():
        finish_plain()
        matmul()

    @pl.when(j == n_col)
    def _():
        _cast_blocks((cg_ref, cu_ref, cd_ref, og_ref, ou_ref, od_ref))
        finish_plain()


def _even_in_proj(x, t, gain, w_bf16, li, qk_gain, cos3, sin3, a_width, experts, layer):
    d = x.shape[-1]
    n = w_bf16.shape[2]
    ns = t // N_STREAM
    bm, bn = 512, 512
    bs = bm // N_STREAM
    n_col = n // bn
    kern = functools.partial(_even_in_kernel, n_qk_blocks=2 * a_width // bn)
    c_views, c_in, c_out, c_shapes = _cast_specs(experts, layer, (t // bm) * (n_col + 1),
                                                 lambda i, j: i * (n_col + 1) + j)
    return pl.pallas_call(
        kern,
        grid=(t // bm, n_col + 1),
        in_specs=[_x_spec(x, bm, d, lambda i, j: (i, 0)),
                  pl.BlockSpec((1, d), lambda i, j: (0, 0)),
                  pl.BlockSpec((None, d, bn), lambda i, j: (li, 0, jnp.minimum(j, n_col - 1))),
                  pl.BlockSpec((2, HEAD_DIM), lambda i, j: (0, 0)),
                  pl.BlockSpec((N_STREAM, bs, HEAD_DIM), lambda i, j: (0, i, 0)),
                  pl.BlockSpec((N_STREAM, bs, HEAD_DIM), lambda i, j: (0, i, 0))] + c_in,
        out_specs=[pl.BlockSpec((N_STREAM, bs, bn), lambda i, j: (0, i, jnp.maximum(j - 1, 0)))] + c_out,
        out_shape=[jax.ShapeDtypeStruct((N_STREAM, ns, n), F32)] + c_shapes,
        scratch_shapes=[pltpu.VMEM((bm, d), F32), pltpu.VMEM((bm, d), BF16), pltpu.VMEM((2, bm, bn), F32)],
        compiler_params=_cparams("arbitrary", "arbitrary"),
        name="even_in_proj",
    )(x, gain[None, :], w_bf16, qk_gain, cos3, sin3, *c_views)


def _attn_kernel(q_ref, k_ref, v_ref, o_ref, o_scr, l_scr):
    sb = pl.program_id(1)
    scale = HEAD_DIM ** -0.5
    base = pl.multiple_of(sb * ATTN_BLOCK, ATTN_BLOCK)
    prev_base = pl.multiple_of(jnp.maximum(base - ATTN_BLOCK, 0), ATTN_BLOCK)
    row = lax.broadcasted_iota(jnp.int32, (ATTN_BLOCK, 2 * ATTN_BLOCK), 0)
    col = lax.broadcasted_iota(jnp.int32, (ATTN_BLOCK, 2 * ATTN_BLOCK), 1)
    is_prev = col < ATTN_BLOCK
    colk = col & (ATTN_BLOCK - 1)

    for pat, d in enumerate(DILATIONS):
        c = N_STREAM // d
        cl = ATTN_BLOCK // c
        sh = cl.bit_length() - 1
        qpos = c * (row & (cl - 1)) + (row >> sh)
        kpos = c * (colk & (cl - 1)) + (colk >> sh)
        bias = jnp.where(is_prev,
                         jnp.where(qpos <= kpos, 0.0, -jnp.inf),
                         jnp.where(qpos >= kpos, 0.0, -jnp.inf)).astype(F32)
        bias_start = jnp.where(jnp.logical_and(is_prev, sb == 0), -jnp.inf, bias)

        for r_d in range(d):
            for jj in range(c):
                lo = jj * cl
                rows = [r_d + d * a for a in range(c)]
                if jj == 0:
                    prev = pl.ds(prev_base + (ATTN_BLOCK - cl), cl)
                else:
                    prev = pl.ds(base + (lo - cl), cl)
                cur = pl.ds(base + lo, cl)
                q = jnp.concatenate([q_ref[r, lo:lo + cl, :] for r in rows], axis=0).astype(BF16)
                k = jnp.concatenate([k_ref[r, prev, :] for r in rows] + [k_ref[r, cur, :] for r in rows],
                                    axis=0).astype(BF16)
                v = jnp.concatenate([v_ref[r, prev, :] for r in rows] + [v_ref[r, cur, :] for r in rows],
                                    axis=0).astype(BF16)
                s = lax.dot_general(q, k, (((1,), (1,)), ((), ())), preferred_element_type=F32) * scale
                s = s + (bias_start if jj == 0 else bias)
                m = jnp.max(s, axis=-1, keepdims=True)
                p = jnp.exp(s - m)
                den = jnp.sum(p, axis=-1, keepdims=True)
                o = jnp.dot(p.astype(BF16), v, preferred_element_type=F32) / den
                lse = jnp.broadcast_to(m + jnp.log(den), (ATTN_BLOCK, HEAD_DIM))
                for a, r in enumerate(rows):
                    o_scr[pat, r, lo:lo + cl, :] = o[a * cl:(a + 1) * cl, :]
                    l_scr[pat, r, lo:lo + cl, :] = lse[a * cl:(a + 1) * cl, :]

    def mix(r, carry):
        l0, l1, l2 = l_scr[0, r], l_scr[1, r], l_scr[2, r]
        mx = jnp.maximum(jnp.maximum(l0, l1), l2)
        w0, w1, w2 = jnp.exp(l0 - mx), jnp.exp(l1 - mx), jnp.exp(l2 - mx)
        num = w0 * o_scr[0, r] + w1 * o_scr[1, r] + w2 * o_scr[2, r]
        o_ref[r] = (num / (w0 + w1 + w2)).astype(o_ref.dtype)
        return carry

    lax.fori_loop(0, N_STREAM, mix, 0)


def _dilated_attention(proj3, n_heads):
    _, ns, n = proj3.shape
    n_sb = ns // ATTN_BLOCK
    out3 = pl.pallas_call(
        _attn_kernel,
        grid=(n_heads, n_sb),
        in_specs=[pl.BlockSpec((N_STREAM, ATTN_BLOCK, HEAD_DIM), lambda h, sb: (0, sb, h)),
                  pl.BlockSpec((N_STREAM, ns, HEAD_DIM), lambda h, sb: (0, 0, n_heads + h)),
                  pl.BlockSpec((N_STREAM, ns, HEAD_DIM), lambda h, sb: (0, 0, 2 * n_heads + h))],
        out_specs=pl.BlockSpec((N_STREAM, ATTN_BLOCK, HEAD_DIM), lambda h, sb: (0, sb, h)),
        out_shape=jax.ShapeDtypeStruct((N_STREAM, ns, n_heads * HEAD_DIM), BF16),
        scratch_shapes=[pltpu.VMEM((3, N_STREAM, ATTN_BLOCK, HEAD_DIM), F32),
                        pltpu.VMEM((3, N_STREAM, ATTN_BLOCK, HEAD_DIM), F32)],
        compiler_params=_cparams("arbitrary", "arbitrary"),
        name="dilated_attention",
    )(proj3, proj3, proj3)
    return out3


def _pool_kernel(pb_ref, w_ref, sc_ref, o_ref, pre_scr):
    g = pl.program_id(0)
    ns, gw = pb_ref.shape[1], pb_ref.shape[2]
    first = lax.broadcasted_iota(jnp.int32, (ns, gw), 0) == 0

    for r in range(N_STREAM):
        pre_scr[r] = pb_ref[r] if r == 0 else pre_scr[r - 1] + pb_ref[r]

    for gi, p in enumerate(POOL_SIZES):
        @pl.when(g == gi)
        def _(p=p):
            for r in range(N_STREAM):
                win = pre_scr[r] - pre_scr[r - p] if r - p >= 0 else pre_scr[r]
                if r - p + 1 < 0:
                    wrap = pre_scr[N_STREAM - 1] - pre_scr[r - p + N_STREAM]
                    win = win + jnp.where(first, 0.0, pltpu.roll(wrap, 1, 0))
                cnt = jnp.where(first, float(min(r + 1, p)), float(p))
                pooled = win / cnt - pb_ref[r]
                y = jnp.dot(pooled.astype(BF16), w_ref[...], preferred_element_type=F32) * sc_ref[...]
                o_ref[r] = y.astype(o_ref.dtype)


def _pooling_mixer(proj3, w_group_bf16, li, scale):
    _, ns, n = proj3.shape
    _, n_groups, gw, _ = w_group_bf16.shape
    pb_blk0 = (n - n_groups * gw) // gw
    return pl.pallas_call(
        _pool_kernel,
        grid=(n_groups,),
        in_specs=[pl.BlockSpec((N_STREAM, ns, gw), lambda g: (0, 0, pb_blk0 + g)),
                  pl.BlockSpec((None, None, gw, gw), lambda g: (li, g, 0, 0)),
                  pl.BlockSpec((1, gw), lambda g: (0, g))],
        out_specs=pl.BlockSpec((N_STREAM, ns, gw), lambda g: (0, 0, g)),
        out_shape=jax.ShapeDtypeStruct((N_STREAM, ns, n_groups * gw), BF16),
        scratch_shapes=[pltpu.VMEM((N_STREAM, ns, gw), F32)],
        compiler_params=_cparams("arbitrary"),
        name="pooling_mixer",
    )(proj3, w_group_bf16, scale[None, :])


def _even_out_kernel(a_ref, b_ref, wa_ref, wb_ref, x_ref, fg_ref, rhi_ref, rlo_ref, rb_ref,
                     o_ref, route_ref, a_scr, b_scr, xn_scr, lg_scr, ss_scr, *, width):
    j = pl.program_id(1)
    n_col = pl.num_programs(1) - 1
    bs = a_ref.shape[1]
    bm = N_STREAM * bs

    def project():
        slot = xn_scr.at[j % 2]
        mix = (jnp.dot(a_scr[...], wa_ref[...], preferred_element_type=F32)
               + jnp.dot(b_scr[...], wb_ref[...], preferred_element_type=F32))
        xn = _rows_2d(x_ref, slot) + mix
        slot[...] = xn
        o_ref[...] = xn.reshape(o_ref.shape)

    def route_prev(last):
        _router_accumulate(xn_scr[(j - 1) % 2], last, fg_ref, rhi_ref, rlo_ref, rb_ref, route_ref,
                           lg_scr, ss_scr, width)

    @pl.when(j == 0)
    def _():
        n = lax.broadcasted_iota(jnp.int32, (bm, bm), 0)
        c = lax.broadcasted_iota(jnp.int32, (bm, bm), 1)
        perm = jnp.where(c == (n % N_STREAM) * bs + n // N_STREAM, 1.0, 0.0).astype(BF16)
        a = a_ref[...].reshape(bm, a_ref.shape[2])
        b = b_ref[...].reshape(bm, b_ref.shape[2])
        a_scr[...] = jnp.dot(perm, a, preferred_element_type=F32).astype(BF16)
        b_scr[...] = jnp.dot(perm, b, preferred_element_type=F32).astype(BF16)
        lg_scr[...] = jnp.zeros(lg_scr.shape, F32)
        ss_scr[...] = jnp.zeros(ss_scr.shape, F32)
        project()

    @pl.when((j >= 1) & (j < n_col))
    def _():
        route_prev(False)
        project()

    @pl.when(j == n_col)
    def _():
        route_prev(True)


def _even_out_proj(oa3, ob3, w_bf16, li, x, t, router):
    d = x.shape[-1]
    ka, kb = oa3.shape[2], ob3.shape[2]
    assert ka % kb == 0
    bm, bn = 512, 1024
    bs = bm // N_STREAM
    n_col = d // bn
    cur = lambda j: jnp.minimum(j, n_col - 1)
    r_in, r_out, r_scr = _router_specs(bm, bn)
    return pl.pallas_call(
        functools.partial(_even_out_kernel, width=d),
        grid=(t // bm, n_col + 1),
        in_specs=[pl.BlockSpec((N_STREAM, bs, ka), lambda i, j: (0, i, 0)),
                  pl.BlockSpec((N_STREAM, bs, kb), lambda i, j: (0, i, 0)),
                  pl.BlockSpec((None, ka, bn), lambda i, j: (li, 0, cur(j))),
                  pl.BlockSpec((None, kb, bn), lambda i, j: (li, ka // kb, cur(j))),
                  _x_spec(x, bm, bn, lambda i, j: (i, cur(j)))] + r_in,
        out_specs=[pl.BlockSpec((bm, 1, bn), lambda i, j: (i, 0, cur(j))), r_out],
        out_shape=[jax.ShapeDtypeStruct((t, 1, d), F32), jax.ShapeDtypeStruct((8, t), F32)],
        scratch_shapes=[pltpu.VMEM((bm, ka), BF16), pltpu.VMEM((bm, kb), BF16), pltpu.VMEM((2, bm, bn), F32)] + r_scr,
        compiler_params=_cparams("arbitrary", "arbitrary"),
        name="even_out_proj",
    )(oa3, ob3, w_bf16, w_bf16, x, *router)


def _odd_in_kernel(x_ref, gain_ref, w_ref, cg_ref, cu_ref, cd_ref, o_ref, og_ref, ou_ref, od_ref, x2_scr, h_scr):
    @pl.when(pl.program_id(1) == 0)
    def _():
        h_scr[...] = _rms(_rows_2d(x_ref, x2_scr), gain_ref[...]).astype(BF16)

    _cast_blocks((cg_ref, cu_ref, cd_ref, og_ref, ou_ref, od_ref))
    acc = jnp.dot(h_scr[...], w_ref[...], preferred_element_type=F32)
    o_ref[...] = jax.nn.gelu(acc).astype(o_ref.dtype)


def _odd_in_proj(x, t, gain, w_bf16, li, experts, layer):
    d = x.shape[-1]
    n = w_bf16.shape[2]
    bm, bn = 512, 512
    n_col = n // bn
    c_views, c_in, c_out, c_shapes = _cast_specs(experts, layer, (t // bm) * n_col, lambda i, j: i * n_col + j)
    return pl.pallas_call(
        _odd_in_kernel,
        grid=(t // bm, n_col),
        in_specs=[_x_spec(x, bm, d, lambda i, j: (i, 0)),
                  pl.BlockSpec((1, d), lambda i, j: (0, 0)),
                  pl.BlockSpec((None, d, bn), lambda i, j: (li, 0, j))] + c_in,
        out_specs=[pl.BlockSpec((bm, bn), lambda i, j: (i, j))] + c_out,
        out_shape=[jax.ShapeDtypeStruct((t, n), BF16)] + c_shapes,
        scratch_shapes=[pltpu.VMEM((bm, d), F32), pltpu.VMEM((bm, d), BF16)],
        compiler_params=_cparams("arbitrary", "arbitrary"),
        name="odd_in_proj",
    )(x, gain[None, :], w_bf16, *c_views)


def _odd_out_kernel(u_ref, v_ref, vg_ref, ws_ref, bcol_ref, w_ref, x_ref, fg_ref, rhi_ref, rlo_ref, rb_ref,
                    o_ref, route_ref, g_scr, xn_scr, lg_scr, ss_scr):
    j = pl.program_id(1)
    n_col = pl.num_programs(1) - 1

    def project():
        slot = xn_scr.at[j % 2]
        xn = _rows_2d(x_ref, slot) + jnp.dot(g_scr[...], w_ref[...], preferred_element_type=F32)
        slot[...] = xn
        o_ref[...] = xn.reshape(o_ref.shape)

    def route_prev(last):
        _router_accumulate(xn_scr[(j - 1) % 2], last, fg_ref, rhi_ref, rlo_ref, rb_ref, route_ref,
                           lg_scr, ss_scr, g_scr.shape[1])

    @pl.when(j == 0)
    def _():
        bm, width = g_scr.shape
        gd = width // GM_GROUPS
        ssq = jnp.zeros((bm, 1), F32)
        for g in range(GM_GROUPS):
            vg = v_ref[:, g * gd:(g + 1) * gd].astype(F32)
            ssq = ssq + jnp.sum(vg * vg, axis=-1, keepdims=True)
        inv = lax.rsqrt(ssq / width + EPS)
        causal = (lax.broadcasted_iota(jnp.int32, (GM_CHUNK, GM_CHUNK), 0)
                  >= lax.broadcasted_iota(jnp.int32, (GM_CHUNK, GM_CHUNK), 1))
        wsum = jnp.where(causal, ws_ref[0], 0.0)
        for g in range(1, GM_GROUPS):
            wsum = wsum + jnp.where(causal, ws_ref[g], 0.0)
        wsum = wsum.astype(BF16)
        for g in range(GM_GROUPS):
            bcol = bcol_ref[:, g:g + 1]
            cols = slice(g * gd, (g + 1) * gd)
            for cc in range(bm // GM_CHUNK):
                rows = slice(cc * GM_CHUNK, (cc + 1) * GM_CHUNK)
                vn = (v_ref[rows, cols].astype(F32) * inv[rows, :] * vg_ref[:, cols]).astype(BF16)
                sv = jnp.dot(wsum, vn, preferred_element_type=F32) + bcol
                g_scr[rows, cols] = (u_ref[rows, cols].astype(F32) * sv).astype(BF16)
        lg_scr[...] = jnp.zeros(lg_scr.shape, F32)
        ss_scr[...] = jnp.zeros(ss_scr.shape, F32)
        project()

    @pl.when((j >= 1) & (j < n_col))
    def _():
        route_prev(False)
        project()

    @pl.when(j == n_col)
    def _():
        route_prev(True)


def _odd_out_proj(z, v_gain, w_spatial, b_spatial, w_bf16, li, x, t, router):
    d = x.shape[-1]
    width = z.shape[1] // 2
    assert width == d
    bm, bn = 512, 1024
    n_col = d // bn
    cur = lambda j: jnp.minimum(j, n_col - 1)
    r_in, r_out, r_scr = _router_specs(bm, bn)
    return pl.pallas_call(
        _odd_out_kernel,
        grid=(t // bm, n_col + 1),
        in_specs=[pl.BlockSpec((bm, width), lambda i, j: (i, 0)),
                  pl.BlockSpec((bm, width), lambda i, j: (i, 1)),
                  pl.BlockSpec((1, width), lambda i, j: (0, 0)),
                  pl.BlockSpec((GM_GROUPS, GM_CHUNK, GM_CHUNK), lambda i, j: (0, 0, 0)),
                  pl.BlockSpec((GM_CHUNK, GM_GROUPS), lambda i, j: (0, 0)),
                  pl.BlockSpec((None, width, bn), lambda i, j: (li, 0, cur(j))),
                  _x_spec(x, bm, bn, lambda i, j: (i, cur(j)))] + r_in,
        out_specs=[pl.BlockSpec((bm, 1, bn), lambda i, j: (i, 0, cur(j))), r_out],
        out_shape=[jax.ShapeDtypeStruct((t, 1, d), F32), jax.ShapeDtypeStruct((8, t), F32)],
        scratch_shapes=[pltpu.VMEM((bm, width), BF16), pltpu.VMEM((2, bm, bn), F32)] + r_scr,
        compiler_params=_cparams("arbitrary", "arbitrary"),
        name="odd_out_proj",
    )(z, z, v_gain[None, :], w_spatial, b_spatial.T, w_bf16, x, *router)


def _router_accumulate(xn, last, gain_ref, whi_ref, wlo_ref, bias_ref, route_ref, lg_scr, ss_scr, width):
    xg = xn * gain_ref[...]
    hi = xg.astype(BF16)
    lo = (xg - hi.astype(F32)).astype(BF16)
    part = (jnp.dot(hi, whi_ref[...], preferred_element_type=F32)
            + jnp.dot(lo, whi_ref[...], preferred_element_type=F32)
            + jnp.dot(hi, wlo_ref[...], preferred_element_type=F32))
    lg_scr[...] = lg_scr[...] + part
    ss_scr[...] = ss_scr[...] + jnp.sum(xn * xn, axis=-1, keepdims=True)

    if last:
        _route_rows(lg_scr[...] * lax.rsqrt(ss_scr[...] / width + EPS), bias_ref, route_ref)


def _route_rows(logits, bias_ref, o_ref):
    lt = logits.T[0:N_EXPERTS, :]
    e = jnp.exp(lt - jnp.max(lt, axis=0, keepdims=True))
    scores = e / jnp.sum(e, axis=0, keepdims=True)
    biased = scores + bias_ref[...]
    sc = [scores[i:i + 1, :] for i in range(N_EXPERTS)]
    bi = [biased[i:i + 1, :] for i in range(N_EXPERTS)]

    def top2_sum(a, b, c, d):
        return jnp.maximum(jnp.maximum(jnp.maximum(a + b, a + c), jnp.maximum(a + d, b + c)),
                           jnp.maximum(b + d, c + d))

    grp = [top2_sum(*bi[EXPERTS_PER_GROUP * g:EXPERTS_PER_GROUP * (g + 1)]) for g in range(N_EXPERT_GROUPS)]
    best = jnp.zeros_like(grp[0], dtype=jnp.int32)
    best_score = grp[0]
    for g in range(1, N_EXPERT_GROUPS):
        upd = grp[g] > best_score
        best = jnp.where(upd, g, best)
        best_score = jnp.where(upd, grp[g], best_score)

    def pick(vals, k):
        out = vals[k]
        for g in range(1, N_EXPERT_GROUPS):
            out = jnp.where(best == g, vals[EXPERTS_PER_GROUP * g + k], out)
        return out

    vb = [pick(bi, k) for k in range(EXPERTS_PER_GROUP)]
    vs = [pick(sc, k) for k in range(EXPERTS_PER_GROUP)]
    i1 = jnp.zeros_like(best)
    m1 = vb[0]
    for k in range(1, EXPERTS_PER_GROUP):
        upd = vb[k] > m1
        i1 = jnp.where(upd, k, i1)
        m1 = jnp.where(upd, vb[k], m1)
    i2 = jnp.zeros_like(best)
    m2 = jnp.full_like(m1, -jnp.inf)
    for k in range(EXPERTS_PER_GROUP):
        upd = (i1 != k) & (vb[k] > m2)
        i2 = jnp.where(upd, k, i2)
        m2 = jnp.where(upd, vb[k], m2)

    def take(idx):
        out = vs[0]
        for k in range(1, EXPERTS_PER_GROUP):
            out = jnp.where(idx == k, vs[k], out)
        return out

    g1, g2 = take(i1), take(i2)
    tot = g1 + g2
    g1, g2 = g1 / tot, g2 / tot
    swap = i2 < i1
    lo = jnp.where(swap, i2, i1)
    hi = jnp.where(swap, i1, i2)
    o_ref[0:1, :] = best.astype(F32)
    o_ref[1:2, :] = lo.astype(F32)
    o_ref[2:3, :] = hi.astype(F32)
    o_ref[3:4, :] = jnp.where(swap, g2, g1)
    o_ref[4:5, :] = jnp.where(swap, g1, g2)
    o_ref[5:8, :] = jnp.zeros((3, best.shape[1]), F32)


def _router_specs(bm, bn):
    prev = lambda j: jnp.maximum(j - 1, 0)
    in_specs = [pl.BlockSpec((1, bn), lambda i, j: (0, prev(j))),
                pl.BlockSpec((bn, HEAD_DIM), lambda i, j: (prev(j), 0)),
                pl.BlockSpec((bn, HEAD_DIM), lambda i, j: (prev(j), 0)),
                pl.BlockSpec((N_EXPERTS, 1), lambda i, j: (0, 0))]
    out_spec = pl.BlockSpec((8, bm), lambda i, j: (0, i))
    scratch = [pltpu.VMEM((bm, HEAD_DIM), F32), pltpu.VMEM((bm, 1), F32)]
    return in_specs, out_spec, scratch


def _moe_plan(route, n_tiles):
    t = route.shape[1]
    i32 = jnp.int32
    best, lo, hi = route[0].astype(i32), route[1].astype(i32), route[2].astype(i32)
    pair = lo * (7 - lo) // 2 + (hi - lo - 1)
    bucket = best * N_PAIRS + pair
    onehot = bucket[:, None] == jnp.arange(N_BUCKETS, dtype=i32)[None, :]
    csum = jnp.cumsum(onehot.astype(i32), axis=0)
    counts = csum[-1]
    tiles = (counts + MOE_TILE - 1) // MOE_TILE
    tile_end = jnp.cumsum(tiles)
    tile_start = tile_end - tiles
    total = tile_end[-1]
    pos = jnp.sum(jnp.where(onehot, csum - 1 + tile_start[None, :] * MOE_TILE, 0), axis=1)
    packed = jnp.stack([jnp.arange(t, dtype=F32), route[3], route[4]], axis=1)
    spare = (t + jnp.arange(n_tiles * MOE_TILE, dtype=i32) % MOE_TILE).astype(F32)
    init = jnp.stack([spare, jnp.zeros_like(spare), jnp.zeros_like(spare)], axis=1)
    rows = init.at[pos].set(packed, unique_indices=True)
    token_of = rows[:, 0].astype(i32)
    gates = rows[:, 1:3]
    tile_ids = jnp.arange(n_tiles, dtype=i32)
    tile_valid = (tile_ids < total).astype(i32)
    tile_bucket = jnp.sum(tile_end[None, :] <= jnp.minimum(tile_ids, total - 1)[:, None], axis=1).astype(i32)
    tile_group = tile_bucket // N_PAIRS
    tile_pair = tile_bucket % N_PAIRS
    pair_lo = (tile_pair >= 3).astype(i32) + (tile_pair >= 5).astype(i32)
    pair_hi = tile_pair - pair_lo * (7 - pair_lo) // 2 + pair_lo + 1
    odd = (tile_ids & 1) == 1
    first = tile_group * EXPERTS_PER_GROUP + jnp.where(odd, pair_hi, pair_lo)
    second = tile_group * EXPERTS_PER_GROUP + jnp.where(odd, pair_lo, pair_hi)
    step_expert = jnp.stack([first, second], axis=1).reshape(-1).astype(i32)
    last = jnp.sum(jnp.where(jnp.arange(2 * n_tiles, dtype=i32) == 2 * total - 1, step_expert, 0))
    step_expert = jnp.where(jnp.repeat(tile_valid, 2) > 0, step_expert, last)
    return step_expert, tile_valid, token_of, gates


def _moe_kernel(se_ref, tv_ref, tok_ref, x_hbm, gates_ref, gain_ref, wg_ref, wu_ref, wd_ref,
                out_hbm, xbuf, x2_scr, hbuf, obuf, sem_in, sem_out):
    del se_ref
    n_tiles = pl.num_programs(0)
    n_tokens = x_hbm.shape[0]
    i = pl.program_id(0)
    s = pl.program_id(1)
    valid = tv_ref[i] > 0
    next_valid = tv_ref[jnp.minimum(i + 1, n_tiles - 1)] * (i + 1 < n_tiles) > 0

    def gather_start(tile):
        def body(r, carry):
            tok = tok_ref[tile * MOE_TILE + r]
            src = jnp.where(tok < n_tokens, tok, 0)
            pltpu.make_async_copy(x_hbm.at[pl.ds(src, 1)], xbuf.at[pl.ds(r, 1)], sem_in).start()
            return carry
        lax.fori_loop(0, MOE_TILE, body, 0, unroll=8)

    def gather_wait():
        pltpu.make_async_copy(x_hbm.at[pl.ds(0, MOE_TILE)], xbuf, sem_in).wait()

    def scatter_start(tile):
        def body(r, carry):
            tok = tok_ref[tile * MOE_TILE + r]
            pltpu.make_async_copy(obuf.at[pl.ds(r, 1)], out_hbm.at[pl.ds(tok, 1)], sem_out).start()
            return carry
        lax.fori_loop(0, MOE_TILE, body, 0, unroll=8)

    def scatter_wait():
        pltpu.make_async_copy(obuf, out_hbm.at[pl.ds(0, MOE_TILE)], sem_out).wait()

    def expert_mlp():
        which = s ^ (i & 1)
        gate = jnp.where(which == 0, gates_ref[:, 0:1], gates_ref[:, 1:2])
        h = hbuf[...]
        hg = jnp.dot(h, wg_ref[...], preferred_element_type=F32)
        hu = jnp.dot(h, wu_ref[...], preferred_element_type=F32)
        act = (jax.nn.silu(hg) * hu * gate).astype(BF16)
        return jnp.dot(act, wd_ref[...], preferred_element_type=F32)

    @pl.when(valid & (s == 0))
    def _():
        @pl.when(i == 0)
        def _():
            gather_start(0)
            obuf[...] = jnp.zeros(obuf.shape, obuf.dtype)

        gather_wait()
        x = _rows_2d(xbuf, x2_scr)
        hbuf[...] = _rms(x, gain_ref[...]).astype(BF16)

        nxt = jnp.minimum(i + 1, n_tiles - 1) * MOE_TILE
        prv = jnp.maximum(i - 1, 0) * MOE_TILE
        for r in range(MOE_TILE):
            tok = tok_ref[nxt + r]
            src = jnp.where(tok < n_tokens, tok, 0)
            pltpu.make_async_copy(x_hbm.at[pl.ds(src, 1)], xbuf.at[pl.ds(r, 1)], sem_in).start()
            dst = jnp.where(i > 0, tok_ref[prv + r], n_tokens + r)
            pltpu.make_async_copy(obuf.at[pl.ds(r, 1)], out_hbm.at[pl.ds(dst, 1)], sem_out).start()

        x2_scr[...] = x2_scr[...] + expert_mlp()

    @pl.when(valid & (s == 1))
    def _():
        @pl.when(jnp.logical_not(next_valid))
        def _():
            gather_wait()

        y = expert_mlp()
        scatter_wait()
        obuf[...] = (x2_scr[...] + y).reshape(obuf.shape)

        @pl.when(jnp.logical_not(next_valid))
        def _():
            scatter_start(i)
            scatter_wait()


def _grouped_moe(x3, route, gain, wg_bf16, wu_bf16, wd_bf16):
    t, _, d = x3.shape
    f = wg_bf16.shape[2]
    n_tiles = t // MOE_TILE + N_BUCKETS
    step_expert, tile_valid, token_of, gates = _moe_plan(route, n_tiles)
    grid_spec = pltpu.PrefetchScalarGridSpec(
        num_scalar_prefetch=3,
        grid=(n_tiles, 2),
        in_specs=[pl.BlockSpec(memory_space=pl.ANY),
                  pl.BlockSpec((MOE_TILE, 2), lambda i, s, se, tv, tok: (i, 0)),
                  pl.BlockSpec((1, d), lambda i, s, se, tv, tok: (0, 0)),
                  pl.BlockSpec((None, d, f), lambda i, s, se, tv, tok: (se[2 * i + s], 0, 0)),
                  pl.BlockSpec((None, d, f), lambda i, s, se, tv, tok: (se[2 * i + s], 0, 0)),
                  pl.BlockSpec((None, f, d), lambda i, s, se, tv, tok: (se[2 * i + s], 0, 0))],
        out_specs=pl.BlockSpec(memory_space=pl.ANY),
        scratch_shapes=[pltpu.VMEM((MOE_TILE, 1, d), F32),
                        pltpu.VMEM((MOE_TILE, d), F32),
                        pltpu.VMEM((MOE_TILE, d), BF16),
                        pltpu.VMEM((MOE_TILE, 1, d), F32),
                        pltpu.SemaphoreType.DMA(()),
                        pltpu.SemaphoreType.DMA(())],
    )
    return pl.pallas_call(
        _moe_kernel,
        grid_spec=grid_spec,
        out_shape=jax.ShapeDtypeStruct((t + MOE_TILE, 1, d), F32),
        compiler_params=_cparams("arbitrary", "arbitrary"),
        name="moe_experts",
    )(step_expert, tile_valid, token_of, x3, gates, gain[None, :], wg_bf16, wu_bf16, wd_bf16)


def kernel(x, positions, norm_mix, norm_ffn, a_w_in, a_q_norm, a_k_norm, b_w_group, b_scale, ab_w_out,
           c_w_in, c_v_norm, c_w_spatial, c_b_spatial, c_w_out, router_w, router_bias,
           expert_w_gate, expert_w_up, expert_w_down):
    batch, t, d = x.shape
    assert batch == 1 and t % (N_STREAM * ATTN_BLOCK) == 0
    depth = norm_mix.shape[0]
    a_width = (a_w_in.shape[2] - b_w_group.shape[1] * b_w_group.shape[2]) // 3
    n_heads = a_width // HEAD_DIM
    ns = t // N_STREAM

    pos_rows = positions[0].reshape(ns, N_STREAM).T.reshape(t)
    cos, sin = (tab.reshape(N_STREAM, ns, HEAD_DIM) for tab in _trig_tables(pos_rows))
    router_w_pad = jnp.pad(router_w, ((0, 0), (0, HEAD_DIM - N_EXPERTS)))
    router_hi = router_w_pad.astype(BF16)
    router_lo = (router_w_pad - router_hi.astype(F32)).astype(BF16)
    a_w_in, b_w_group, ab_w_out, c_w_in, c_w_out = (
        w.astype(BF16) for w in (a_w_in, b_w_group, ab_w_out, c_w_in, c_w_out))
    experts = (expert_w_gate, expert_w_up, expert_w_down)
    n_experts, _, d_expert = expert_w_gate.shape[1:]

    xs = x[0]
    for layer in range(depth):
        i = layer // 2
        router = (norm_ffn[layer][None, :], router_hi, router_lo, router_bias[:, None])
        if layer % 2 == 0:
            proj3, wg, wu, wd = _even_in_proj(xs, t, norm_mix[layer], a_w_in, i,
                                              jnp.stack([a_q_norm[i], a_k_norm[i]]), cos, sin, a_width,
                                              experts, layer)
            oa3 = _dilated_attention(proj3, n_heads)
            ob3 = _pooling_mixer(proj3, b_w_group, i, b_scale[i])
            x3, route = _even_out_proj(oa3, ob3, ab_w_out, i, xs, t, router)
        else:
            z, wg, wu, wd = _odd_in_proj(xs, t, norm_mix[layer], c_w_in, i, experts, layer)
            x3, route = _odd_out_proj(z, c_v_norm[i], c_w_spatial[i], c_b_spatial[i], c_w_out, i, xs, t, router)
        xs = _grouped_moe(x3, route, norm_ffn[layer],
                          wg.reshape(n_experts, d, d_expert), wu.reshape(n_experts, d, d_expert),
                          wd.reshape(n_experts, d_expert, d))
    return xs[:t].reshape(1, t, d)
```

```python
import functools

import jax
import jax.numpy as jnp
from jax import lax
from jax.experimental import pallas as pl
from jax.experimental.pallas import tpu as pltpu

F32 = jnp.float32
BF16 = jnp.bfloat16

EPS = 1e-6
HEAD_DIM = 128
ROPE_DIM = HEAD_DIM // 4
ROPE_HALF = ROPE_DIM // 2
ROPE_THETA = 500000.0
POOL_SIZES = (2, 4, 8, 16)
N_STREAM = 16
ATTN_BLOCK = 128
DILATIONS = (16, 4, 1)
GM_GROUPS = 8
GM_CHUNK = 128
N_EXPERTS = 16
N_EXPERT_GROUPS = 4
EXPERTS_PER_GROUP = 4
N_PAIRS = 6
N_BUCKETS = N_EXPERT_GROUPS * N_PAIRS
MOE_TILE = 256
V7X_VMEM_LIMIT = 56 * 1024 * 1024


def _cparams(*sem):
    return pltpu.CompilerParams(dimension_semantics=sem, vmem_limit_bytes=V7X_VMEM_LIMIT)


def _rms(xf, gain_row):
    ms = jnp.mean(xf * xf, axis=-1, keepdims=True)
    return xf * lax.rsqrt(ms + EPS) * gain_row


def _rows_2d(x_ref, x2_scr):
    if len(x_ref.shape) == 2:
        return x_ref[...]
    x2_scr[...] = x_ref[...].reshape(x2_scr.shape)
    return x2_scr[...]


def _cast_specs(weights, layer, n_steps, step_of):
    n_blk = 1
    while n_blk * 2 <= min(n_steps, 128):
        n_blk *= 2
    views, in_specs, out_specs, out_shapes = [], [], [], []
    for w in weights:
        n_layers, e, a, b = w.shape
        rows = e * a // n_blk
        assert rows * n_blk == e * a and rows % 16 == 0
        blk = lambda *g: jnp.minimum(step_of(*g), n_blk - 1)
        views.append(w.reshape(n_layers, e * a, b))
        in_specs.append(pl.BlockSpec((None, rows, b), lambda *g, blk=blk: (layer, blk(*g), 0)))
        out_specs.append(pl.BlockSpec((rows, b), lambda *g, blk=blk: (blk(*g), 0)))
        out_shapes.append(jax.ShapeDtypeStruct((e * a, b), BF16))
    return views, in_specs, out_specs, out_shapes


def _cast_blocks(refs):
    n = len(refs) // 2
    for src, dst in zip(refs[:n], refs[n:]):
        dst[...] = src[...].astype(dst.dtype)


def _x_spec(x, bm, bn, index):
    if x.ndim == 2:
        return pl.BlockSpec((bm, bn), lambda *g: index(*g))
    return pl.BlockSpec((bm, 1, bn), lambda *g: (index(*g)[0], 0, index(*g)[1]))


def _trig_kernel(pos_ref, freq_ref, cos_ref, sin_ref):
    ang = pos_ref[...].astype(F32) * freq_ref[...]
    lane = lax.broadcasted_iota(jnp.int32, ang.shape, 1)
    c = jnp.cos(ang)
    s = jnp.sin(ang)
    cos_ref[...] = jnp.where(lane < ROPE_DIM, c, 1.0)
    sin_ref[...] = jnp.where(lane < ROPE_HALF, -s, jnp.where(lane < ROPE_DIM, s, 0.0))


def _trig_tables(pos_rows):
    t = pos_rows.shape[0]
    inv_freq = ROPE_THETA ** (-jnp.arange(ROPE_HALF, dtype=F32) / ROPE_HALF)
    freq = jnp.tile(inv_freq, HEAD_DIM // ROPE_HALF)[None, :]
    pos_b = jnp.broadcast_to(pos_rows[:, None], (t, HEAD_DIM))
    bm = 1024
    return pl.pallas_call(
        _trig_kernel,
        grid=(t // bm,),
        in_specs=[pl.BlockSpec((bm, HEAD_DIM), lambda i: (i, 0)),
                  pl.BlockSpec((1, HEAD_DIM), lambda i: (0, 0))],
        out_specs=[pl.BlockSpec((bm, HEAD_DIM), lambda i: (i, 0))] * 2,
        out_shape=[jax.ShapeDtypeStruct((t, HEAD_DIM), F32)] * 2,
        compiler_params=_cparams("arbitrary"),
        name="rope_tables",
    )(pos_b, freq)


def _even_in_kernel(x_ref, gain_ref, w_ref, qkg_ref, cos_ref, sin_ref, cg_ref, cu_ref, cd_ref,
                    o_ref, og_ref, ou_ref, od_ref, x2_scr, h_scr, acc_scr, *, n_qk_blocks):
    j = pl.program_id(1)
    n_col = pl.num_programs(1) - 1

    bs = o_ref.shape[1]
    bm = N_STREAM * bs

    def matmul():
        _cast_blocks((cg_ref, cu_ref, cd_ref, og_ref, ou_ref, od_ref))
        acc_scr[j % 2] = jnp.dot(h_scr[...], w_ref[...], preferred_element_type=F32)

    def finish_qk():
        prev = acc_scr.at[(j - 1) % 2]
        is_q = j - 1 < n_qk_blocks // 2
        gain = jnp.where(is_q, qkg_ref[0:1, :], qkg_ref[1:2, :])
        cos = cos_ref[...].reshape(bm, HEAD_DIM) * jnp.where(is_q, HEAD_DIM ** -0.5, 1.0)
        sin = sin_ref[...].reshape(bm, HEAD_DIM) * jnp.where(is_q, HEAD_DIM ** -0.5, 1.0)
        lane = lax.broadcasted_iota(jnp.int32, cos.shape, 1)
        for hh in range(prev.shape[1] // HEAD_DIM):
            y = _rms(prev[:, hh * HEAD_DIM:(hh + 1) * HEAD_DIM], gain)
            swapped = jnp.where(lane < ROPE_HALF,
                                pltpu.roll(y, HEAD_DIM - ROPE_HALF, 1),
                                pltpu.roll(y, ROPE_HALF, 1))
            o_ref[:, :, hh * HEAD_DIM:(hh + 1) * HEAD_DIM] = (y * cos + swapped * sin).reshape(N_STREAM, bs, HEAD_DIM)

    def finish_plain():
        o_ref[...] = acc_scr[(j - 1) % 2].reshape(o_ref.shape)

    @pl.when(j == 0)
    def _():
        i = lax.broadcasted_iota(jnp.int32, (bm, bm), 0)
        c = lax.broadcasted_iota(jnp.int32, (bm, bm), 1)
        perm = jnp.where(c == (i % bs) * N_STREAM + i // bs, 1.0, 0.0).astype(BF16)
        h = _rms(_rows_2d(x_ref, x2_scr), gain_ref[...]).astype(BF16)
        h_scr[...] = jnp.dot(perm, h, preferred_element_type=F32).astype(BF16)
        matmul()

    @pl.when((j >= 1) & (j <= n_qk_blocks))
    def _():
        finish_qk()
        matmul()

    @pl.when((j > n_qk_blocks) & (j < n_col))
    def _():
        finish_plain()
        matmul()

    @pl.when(j == n_col)
    def _():
        _cast_blocks((cg_ref, cu_ref, cd_ref, og_ref, ou_ref, od_ref))
        finish_plain()


def _even_in_proj(x, t, gain, w_bf16, li, qk_gain, cos3, sin3, a_width, experts, layer):
    d = x.shape[-1]
    n = w_bf16.shape[2]
    ns = t // N_STREAM
    bm, bn = 512, 512
    bs = bm // N_STREAM
    n_col = n // bn
    kern = functools.partial(_even_in_kernel, n_qk_blocks=2 * a_width // bn)
    c_views, c_in, c_out, c_shapes = _cast_specs(experts, layer, (t // bm) * (n_col + 1),
                                                 lambda i, j: i * (n_col + 1) + j)
    return pl.pallas_call(
        kern,
        grid=(t // bm, n_col + 1),
        in_specs=[_x_spec(x, bm, d, lambda i, j: (i, 0)),
                  pl.BlockSpec((1, d), lambda i, j: (0, 0)),
                  pl.BlockSpec((None, d, bn), lambda i, j: (li, 0, jnp.minimum(j, n_col - 1))),
                  pl.BlockSpec((2, HEAD_DIM), lambda i, j: (0, 0)),
                  pl.BlockSpec((N_STREAM, bs, HEAD_DIM), lambda i, j: (0, i, 0)),
                  pl.BlockSpec((N_STREAM, bs, HEAD_DIM), lambda i, j: (0, i, 0))] + c_in,
        out_specs=[pl.BlockSpec((N_STREAM, bs, bn), lambda i, j: (0, i, jnp.maximum(j - 1, 0)))] + c_out,
        out_shape=[jax.ShapeDtypeStruct((N_STREAM, ns, n), F32)] + c_shapes,
        scratch_shapes=[pltpu.VMEM((bm, d), F32), pltpu.VMEM((bm, d), BF16), pltpu.VMEM((2, bm, bn), F32)],
        compiler_params=_cparams("arbitrary", "arbitrary"),
        name="even_in_proj",
    )(x, gain[None, :], w_bf16, qk_gain, cos3, sin3, *c_views)


def _attn_kernel(q_ref, k_ref, v_ref, cg_ref, cu_ref, cd_ref, o_ref, og_ref, ou_ref, od_ref, o_scr, l_scr):
    _cast_blocks((cg_ref, cu_ref, cd_ref, og_ref, ou_ref, od_ref))
    sb = pl.program_id(1)
    base =pl.multiple_of(sb * ATTN_BLOCK, ATTN_BLOCK)
    prev_base = pl.multiple_of(jnp.maximum(base - ATTN_BLOCK, 0), ATTN_BLOCK)
    row = lax.broadcasted_iota(jnp.int32, (ATTN_BLOCK, 2 * ATTN_BLOCK), 0)
    col = lax.broadcasted_iota(jnp.int32, (ATTN_BLOCK, 2 * ATTN_BLOCK), 1)
    is_prev = col < ATTN_BLOCK
    colk = col & (ATTN_BLOCK - 1)

    for pat, d in enumerate(DILATIONS):
        c = N_STREAM // d
        cl = ATTN_BLOCK // c
        sh = cl.bit_length() - 1
        qpos = c * (row & (cl - 1)) + (row >> sh)
        kpos = c * (colk & (cl - 1)) + (colk >> sh)
        bias = jnp.where(is_prev,
                         jnp.where(qpos <= kpos, 0.0, -jnp.inf),
                         jnp.where(qpos >= kpos, 0.0, -jnp.inf)).astype(F32)
        bias_start = jnp.where(jnp.logical_and(is_prev, sb == 0), -jnp.inf, bias)

        for r_d in range(d):
            for jj in range(c):
                lo = jj * cl
                rows = [r_d + d * a for a in range(c)]
                if jj == 0:
                    prev = pl.ds(prev_base + (ATTN_BLOCK - cl), cl)
                else:
                    prev = pl.ds(base + (lo - cl), cl)
                cur = pl.ds(base + lo, cl)
                q = jnp.concatenate([q_ref[r, lo:lo + cl, :] for r in rows], axis=0).astype(BF16)
                k = jnp.concatenate([k_ref[r, prev, :] for r in rows] + [k_ref[r, cur, :] for r in rows],
                                    axis=0).astype(BF16)
                v = jnp.concatenate([v_ref[r, prev, :] for r in rows] + [v_ref[r, cur, :] for r in rows],
                                    axis=0).astype(BF16)
                s = lax.dot_general(q, k, (((1,), (1,)), ((), ())), preferred_element_type=F32)
                s = s + (bias_start if jj == 0 else bias)
                m = jnp.max(s, axis=-1, keepdims=True)
                p = jnp.exp(s - m)
                den = jnp.sum(p, axis=-1, keepdims=True)
                o = jnp.dot(p.astype(BF16), v, preferred_element_type=F32) / den
                lse = jnp.broadcast_to(m + jnp.log(den), (ATTN_BLOCK, HEAD_DIM))
                for a, r in enumerate(rows):
                    o_scr[pat, r, lo:lo + cl, :] = o[a * cl:(a + 1) * cl, :]
                    l_scr[pat, r, lo:lo + cl, :] = lse[a * cl:(a + 1) * cl, :]

    for r in range(N_STREAM):
        l0, l1, l2 = l_scr[0, r], l_scr[1, r], l_scr[2, r]
        mx = jnp.maximum(jnp.maximum(l0, l1), l2)
        w0, w1, w2 = jnp.exp(l0 - mx), jnp.exp(l1 - mx), jnp.exp(l2 - mx)
        num = w0 * o_scr[0, r] + w1 * o_scr[1, r] + w2 * o_scr[2, r]
        o_ref[r] = (num / (w0 + w1 + w2)).astype(o_ref.dtype)


def _dilated_attention(proj3, n_heads, experts, cast_layer):
    _, ns, n = proj3.shape
    n_sb = ns // ATTN_BLOCK
    c_views, c_in, c_out, c_shapes = _cast_specs(experts, cast_layer, n_heads * n_sb, lambda h, sb: h * n_sb + sb)
    return pl.pallas_call(
        _attn_kernel,
        grid=(n_heads, n_sb),
        in_specs=[pl.BlockSpec((N_STREAM, ATTN_BLOCK, HEAD_DIM), lambda h, sb: (0, sb, h)),
                  pl.BlockSpec((N_STREAM, ns, HEAD_DIM), lambda h, sb: (0, 0, n_heads + h)),
                  pl.BlockSpec((N_STREAM, ns, HEAD_DIM), lambda h, sb: (0, 0, 2 * n_heads + h))] + c_in,
        out_specs=[pl.BlockSpec((N_STREAM, ATTN_BLOCK, HEAD_DIM), lambda h, sb: (0, sb, h))] + c_out,
        out_shape=[jax.ShapeDtypeStruct((N_STREAM, ns, n_heads * HEAD_DIM), BF16)] + c_shapes,
        scratch_shapes=[pltpu.VMEM((3, N_STREAM, ATTN_BLOCK, HEAD_DIM), F32),
                        pltpu.VMEM((3, N_STREAM, ATTN_BLOCK, HEAD_DIM), F32)],
        compiler_params=_cparams("arbitrary", "arbitrary"),
        name="dilated_attention",
    )(proj3, proj3, proj3, *c_views)


def _pool_kernel(pb_ref, w_ref, sc_ref, o_ref, pre_scr):
    g = pl.program_id(0)
    ns, gw = pb_ref.shape[1], pb_ref.shape[2]
    first = lax.broadcasted_iota(jnp.int32, (ns, gw), 0) == 0

    for r in range(N_STREAM):
        pre_scr[r] = pb_ref[r] if r == 0 else pre_scr[r - 1] + pb_ref[r]

    for gi, p in enumerate(POOL_SIZES):
        @pl.when(g == gi)
        def _(p=p):
            for r in range(N_STREAM):
                win = pre_scr[r] - pre_scr[r - p] if r - p >= 0 else pre_scr[r]
                if r - p + 1 < 0:
                    wrap = pre_scr[N_STREAM - 1] - pre_scr[r - p + N_STREAM]
                    win = win + jnp.where(first, 0.0, pltpu.roll(wrap, 1, 0))
                cnt = jnp.where(first, float(min(r + 1, p)), float(p))
                pooled = win / cnt - pb_ref[r]
                y = jnp.dot(pooled.astype(BF16), w_ref[...], preferred_element_type=F32) * sc_ref[...]
                o_ref[r] = y.astype(o_ref.dtype)


def _pooling_mixer(proj3, w_group_bf16, li, scale):
    _, ns, n = proj3.shape
    _, n_groups, gw, _ = w_group_bf16.shape
    pb_blk0 = (n - n_groups * gw) // gw
    return pl.pallas_call(
        _pool_kernel,
        grid=(n_groups,),
        in_specs=[pl.BlockSpec((N_STREAM, ns, gw), lambda g: (0, 0, pb_blk0 + g)),
                  pl.BlockSpec((None, None, gw, gw), lambda g: (li, g, 0, 0)),
                  pl.BlockSpec((1, gw), lambda g: (0, g))],
        out_specs=pl.BlockSpec((N_STREAM, ns, gw), lambda g: (0, 0, g)),
        out_shape=jax.ShapeDtypeStruct((N_STREAM, ns, n_groups * gw), BF16),
        scratch_shapes=[pltpu.VMEM((N_STREAM, ns, gw), F32)],
        compiler_params=_cparams("arbitrary"),
        name="pooling_mixer",
    )(proj3, w_group_bf16, scale[None, :])


def _even_out_kernel(a_ref, b_ref, wa_ref, wb_ref, x_ref, fg_ref, rhi_ref, rlo_ref, rb_ref,
                     o_ref, route_ref, a_scr, b_scr, xn_scr, lg_scr, ss_scr, *, width):
    j = pl.program_id(1)
    n_col = pl.num_programs(1) - 1
    bs = a_ref.shape[1]
    bm = N_STREAM * bs

    def project():
        slot = xn_scr.at[j % 2]
        mix = (jnp.dot(a_scr[...], wa_ref[...], preferred_element_type=F32)
               + jnp.dot(b_scr[...], wb_ref[...], preferred_element_type=F32))
        xn = _rows_2d(x_ref, slot) + mix
        slot[...] = xn
        o_ref[...] = xn.reshape(o_ref.shape)

    def route_prev(last):
        _router_accumulate(xn_scr[(j - 1) % 2], last, fg_ref, rhi_ref, rlo_ref, rb_ref, route_ref,
                           lg_scr, ss_scr, width)

    @pl.when(j == 0)
    def _():
        n = lax.broadcasted_iota(jnp.int32, (bm, bm), 0)
        c = lax.broadcasted_iota(jnp.int32, (bm, bm), 1)
        perm = jnp.where(c == (n % N_STREAM) * bs + n // N_STREAM, 1.0, 0.0).astype(BF16)
        a = a_ref[...].reshape(bm, a_ref.shape[2])
        b = b_ref[...].reshape(bm, b_ref.shape[2])
        a_scr[...] = jnp.dot(perm, a, preferred_element_type=F32).astype(BF16)
        b_scr[...] = jnp.dot(perm, b, preferred_element_type=F32).astype(BF16)
        lg_scr[...] = jnp.zeros(lg_scr.shape, F32)
        ss_scr[...] = jnp.zeros(ss_scr.shape, F32)
        project()

    @pl.when((j >= 1) & (j < n_col))
    def _():
        route_prev(False)
        project()

    @pl.when(j == n_col)
    def _():
        route_prev(True)


def _even_out_proj(oa3, ob3, w_bf16, li, x, t, router):
    d = x.shape[-1]
    ka, kb = oa3.shape[2], ob3.shape[2]
    assert ka % kb == 0
    bm, bn = 512, 1024
    bs = bm // N_STREAM
    n_col = d // bn
    cur = lambda j: jnp.minimum(j, n_col - 1)
    r_in, r_out, r_scr = _router_specs(bm, bn)
    return pl.pallas_call(
        functools.partial(_even_out_kernel, width=d),
        grid=(t // bm, n_col + 1),
        in_specs=[pl.BlockSpec((N_STREAM, bs, ka), lambda i, j: (0, i, 0)),
                  pl.BlockSpec((N_STREAM, bs, kb), lambda i, j: (0, i, 0)),
                  pl.BlockSpec((None, ka, bn), lambda i, j: (li, 0, cur(j))),
                  pl.BlockSpec((None, kb, bn), lambda i, j: (li, ka // kb, cur(j))),
                  _x_spec(x, bm, bn, lambda i, j: (i, cur(j)))] + r_in,
        out_specs=[pl.BlockSpec((bm, 1, bn), lambda i, j: (i, 0, cur(j))), r_out],
        out_shape=[jax.ShapeDtypeStruct((t, 1, d), F32), jax.ShapeDtypeStruct((8, t), F32)],
        scratch_shapes=[pltpu.VMEM((bm, ka), BF16), pltpu.VMEM((bm, kb), BF16), pltpu.VMEM((2, bm, bn), F32)] + r_scr,
        compiler_params=_cparams("arbitrary", "arbitrary"),
        name="even_out_proj",
    )(oa3, ob3, w_bf16, w_bf16, x, *router)


def _odd_in_kernel(x_ref, gain_ref, w_ref, o_ref, x2_scr, h_scr):
    @pl.when(pl.program_id(1) == 0)
    def _():
        h_scr[...] = _rms(_rows_2d(x_ref, x2_scr), gain_ref[...]).astype(BF16)

    acc = jnp.dot(h_scr[...], w_ref[...], preferred_element_type=F32)
    o_ref[...] = jax.nn.gelu(acc).astype(o_ref.dtype)


def _odd_in_proj(x, t, gain, w_bf16, li):
    d = x.shape[-1]
    n = w_bf16.shape[2]
    bm, bn = 512, 1024
    return pl.pallas_call(
        _odd_in_kernel,
        grid=(t // bm, n // bn),
        in_specs=[_x_spec(x, bm, d, lambda i, j: (i, 0)),
                  pl.BlockSpec((1, d), lambda i, j: (0, 0)),
                  pl.BlockSpec((None, d, bn), lambda i, j: (li, 0, j))],
        out_specs=pl.BlockSpec((bm, bn), lambda i, j: (i, j)),
        out_shape=jax.ShapeDtypeStruct((t, n), BF16),
        scratch_shapes=[pltpu.VMEM((bm, d), F32), pltpu.VMEM((bm, d), BF16)],
        compiler_params=_cparams("arbitrary", "arbitrary"),
        name="odd_in_proj",
    )(x, gain[None, :], w_bf16)


def _odd_out_kernel(u_ref, v_ref, vg_ref, ws_ref, bcol_ref, w_ref, x_ref, fg_ref, rhi_ref, rlo_ref, rb_ref,
                    o_ref, route_ref, g_scr, xn_scr, lg_scr, ss_scr):
    j = pl.program_id(1)
    n_col = pl.num_programs(1) - 1

    def project():
        slot = xn_scr.at[j % 2]
        xn = _rows_2d(x_ref, slot) + jnp.dot(g_scr[...], w_ref[...], preferred_element_type=F32)
        slot[...] = xn
        o_ref[...] = xn.reshape(o_ref.shape)

    def route_prev(last):
        _router_accumulate(xn_scr[(j - 1) % 2], last, fg_ref, rhi_ref, rlo_ref, rb_ref, route_ref,
                           lg_scr, ss_scr, g_scr.shape[1])

    @pl.when(j == 0)
    def _():
        bm, width = g_scr.shape
        gd = width // GM_GROUPS
        ssq = jnp.zeros((bm, 1), F32)
        for g in range(GM_GROUPS):
            vg = v_ref[:, g * gd:(g + 1) * gd].astype(F32)
            ssq = ssq + jnp.sum(vg * vg, axis=-1, keepdims=True)
        inv = lax.rsqrt(ssq / width + EPS)
        causal = (lax.broadcasted_iota(jnp.int32, (GM_CHUNK, GM_CHUNK), 0)
                  >= lax.broadcasted_iota(jnp.int32, (GM_CHUNK, GM_CHUNK), 1))
        wsum = jnp.where(causal, ws_ref[0], 0.0)
        for g in range(1, GM_GROUPS):
            wsum = wsum + jnp.where(causal, ws_ref[g], 0.0)
        wsum = wsum.astype(BF16)
        for g in range(GM_GROUPS):
            bcol = bcol_ref[:, g:g + 1]
            cols = slice(g * gd, (g + 1) * gd)
            for cc in range(bm // GM_CHUNK):
                rows = slice(cc * GM_CHUNK, (cc + 1) * GM_CHUNK)
                vn = (v_ref[rows, cols].astype(F32) * inv[rows, :] * vg_ref[:, cols]).astype(BF16)
                sv = jnp.dot(wsum, vn, preferred_element_type=F32) + bcol
                g_scr[rows, cols] = (u_ref[rows, cols].astype(F32) * sv).astype(BF16)
        lg_scr[...] = jnp.zeros(lg_scr.shape, F32)
        ss_scr[...] = jnp.zeros(ss_scr.shape, F32)
        project()

    @pl.when((j >= 1) & (j < n_col))
    def _():
        route_prev(False)
        project()

    @pl.when(j == n_col)
    def _():
        route_prev(True)


def _odd_out_proj(z, v_gain, w_spatial, b_spatial, w_bf16, li, x, t, router):
    d = x.shape[-1]
    width = z.shape[1] // 2
    assert width == d
    bm, bn = 512, 1024
    n_col = d // bn
    cur = lambda j: jnp.minimum(j, n_col - 1)
    r_in, r_out, r_scr = _router_specs(bm, bn)
    return pl.pallas_call(
        _odd_out_kernel,
        grid=(t // bm, n_col + 1),
        in_specs=[pl.BlockSpec((bm, width), lambda i, j: (i, 0)),
                  pl.BlockSpec((bm, width), lambda i, j: (i, 1)),
                  pl.BlockSpec((1, width), lambda i, j: (0, 0)),
                  pl.BlockSpec((GM_GROUPS, GM_CHUNK, GM_CHUNK), lambda i, j: (0, 0, 0)),
                  pl.BlockSpec((GM_CHUNK, GM_GROUPS), lambda i, j: (0, 0)),
                  pl.BlockSpec((None, width, bn), lambda i, j: (li, 0, cur(j))),
                  _x_spec(x, bm, bn, lambda i, j: (i, cur(j)))] + r_in,
        out_specs=[pl.BlockSpec((bm, 1, bn), lambda i, j: (i, 0, cur(j))), r_out],
        out_shape=[jax.ShapeDtypeStruct((t, 1, d), F32), jax.ShapeDtypeStruct((8, t), F32)],
        scratch_shapes=[pltpu.VMEM((bm, width), BF16), pltpu.VMEM((2, bm, bn), F32)] + r_scr,
        compiler_params=_cparams("arbitrary", "arbitrary"),
        name="odd_out_proj",
    )(z, z, v_gain[None, :], w_spatial, b_spatial.T, w_bf16, x, *router)


def _router_accumulate(xn, last, gain_ref, whi_ref, wlo_ref, bias_ref, route_ref, lg_scr, ss_scr, width):
    xg = xn * gain_ref[...]
    hi = xg.astype(BF16)
    lo = (xg - hi.astype(F32)).astype(BF16)
    part = (jnp.dot(hi, whi_ref[...], preferred_element_type=F32)
            + jnp.dot(lo, whi_ref[...], preferred_element_type=F32)
            + jnp.dot(hi, wlo_ref[...], preferred_element_type=F32))
    lg_scr[...] = lg_scr[...] + part
    ss_scr[...] = ss_scr[...] + jnp.sum(xn * xn, axis=-1, keepdims=True)

    if last:
        _route_rows(lg_scr[...] * lax.rsqrt(ss_scr[...] / width + EPS), bias_ref, route_ref)


def _route_rows(logits, bias_ref, o_ref):
    lt = logits.T[0:N_EXPERTS, :]
    e = jnp.exp(lt - jnp.max(lt, axis=0, keepdims=True))
    scores = e / jnp.sum(e, axis=0, keepdims=True)
    biased = scores + bias_ref[...]
    sc = [scores[i:i + 1, :] for i in range(N_EXPERTS)]
    bi = [biased[i:i + 1, :] for i in range(N_EXPERTS)]

    def top2_sum(a, b, c, d):
        return jnp.maximum(jnp.maximum(jnp.maximum(a + b, a + c), jnp.maximum(a + d, b + c)),
                           jnp.maximum(b + d, c + d))

    grp = [top2_sum(*bi[EXPERTS_PER_GROUP * g:EXPERTS_PER_GROUP * (g + 1)]) for g in range(N_EXPERT_GROUPS)]
    best = jnp.zeros_like(grp[0], dtype=jnp.int32)
    best_score = grp[0]
    for g in range(1, N_EXPERT_GROUPS):
        upd = grp[g] > best_score
        best = jnp.where(upd, g, best)
        best_score = jnp.where(upd, grp[g], best_score)

    def pick(vals, k):
        out = vals[k]
        for g in range(1, N_EXPERT_GROUPS):
            out = jnp.where(best == g, vals[EXPERTS_PER_GROUP * g + k], out)
        return out

    vb = [pick(bi, k) for k in range(EXPERTS_PER_GROUP)]
    vs = [pick(sc, k) for k in range(EXPERTS_PER_GROUP)]
    i1 = jnp.zeros_like(best)
    m1 = vb[0]
    for k in range(1, EXPERTS_PER_GROUP):
        upd = vb[k] > m1
        i1 = jnp.where(upd, k, i1)
        m1 = jnp.where(upd, vb[k], m1)
    i2 = jnp.zeros_like(best)
    m2 = jnp.full_like(m1, -jnp.inf)
    for k in range(EXPERTS_PER_GROUP):
        upd = (i1 != k) & (vb[k] > m2)
        i2 = jnp.where(upd, k, i2)
        m2 = jnp.where(upd, vb[k], m2)

    def take(idx):
        out = vs[0]
        for k in range(1, EXPERTS_PER_GROUP):
            out = jnp.where(idx == k, vs[k], out)
        return out

    g1, g2 = take(i1), take(i2)
    tot = g1 + g2
    g1, g2 = g1 / tot, g2 / tot
    swap = i2 < i1
    lo = jnp.where(swap, i2, i1)
    hi = jnp.where(swap, i1, i2)
    o_ref[0:1, :] = best.astype(F32)
    o_ref[1:2, :] = lo.astype(F32)
    o_ref[2:3, :] = hi.astype(F32)
    o_ref[3:4, :] = jnp.where(swap, g2, g1)
    o_ref[4:5, :] = jnp.where(swap, g1, g2)
    o_ref[5:8, :] = jnp.zeros((3, best.shape[1]), F32)


def _router_specs(bm, bn):
    prev = lambda j: jnp.maximum(j - 1, 0)
    in_specs = [pl.BlockSpec((1, bn), lambda i, j: (0, prev(j))),
                pl.BlockSpec((bn, HEAD_DIM), lambda i, j: (prev(j), 0)),
                pl.BlockSpec((bn, HEAD_DIM), lambda i, j: (prev(j), 0)),
                pl.BlockSpec((N_EXPERTS, 1), lambda i, j: (0, 0))]
    out_spec = pl.BlockSpec((8, bm), lambda i, j: (0, i))
    scratch = [pltpu.VMEM((bm, HEAD_DIM), F32), pltpu.VMEM((bm, 1), F32)]
    return in_specs, out_spec, scratch


def _moe_plan(route, n_tiles):
    t = route.shape[1]
    i32 = jnp.int32
    best, lo, hi = route[0].astype(i32), route[1].astype(i32), route[2].astype(i32)
    pair = lo * (7 - lo) // 2 + (hi - lo - 1)
    bucket = best * N_PAIRS + pair
    onehot = bucket[:, None] == jnp.arange(N_BUCKETS, dtype=i32)[None, :]
    csum = jnp.cumsum(onehot.astype(i32), axis=0)
    counts = csum[-1]
    tiles = (counts + MOE_TILE - 1) // MOE_TILE
    tile_end = jnp.cumsum(tiles)
    tile_start = tile_end - tiles
    total = tile_end[-1]
    pos = jnp.sum(jnp.where(onehot, csum - 1 + tile_start[None, :] * MOE_TILE, 0), axis=1)
    packed = jnp.stack([jnp.arange(t, dtype=F32), route[3], route[4]], axis=1)
    spare = (t + jnp.arange(n_tiles * MOE_TILE, dtype=i32) % MOE_TILE).astype(F32)
    init = jnp.stack([spare, jnp.zeros_like(spare), jnp.zeros_like(spare)], axis=1)
    rows = init.at[pos].set(packed, unique_indices=True)
    token_of = rows[:, 0].astype(i32)
    gates = rows[:, 1:3]
    tile_ids = jnp.arange(n_tiles, dtype=i32)
    tile_valid = (tile_ids < total).astype(i32)
    tile_bucket = jnp.sum(tile_end[None, :] <= jnp.minimum(tile_ids, total - 1)[:, None], axis=1).astype(i32)
    tile_group = tile_bucket // N_PAIRS
    tile_pair = tile_bucket % N_PAIRS
    pair_lo = (tile_pair >= 3).astype(i32) + (tile_pair >= 5).astype(i32)
    pair_hi = tile_pair - pair_lo * (7 - pair_lo) // 2 + pair_lo + 1
    odd = (tile_ids & 1) == 1
    first = tile_group * EXPERTS_PER_GROUP + jnp.where(odd, pair_hi, pair_lo)
    second = tile_group * EXPERTS_PER_GROUP + jnp.where(odd, pair_lo, pair_hi)
    step_expert = jnp.stack([first, second], axis=1).reshape(-1).astype(i32)
    last = jnp.sum(jnp.where(jnp.arange(2 * n_tiles, dtype=i32) == 2 * total - 1, step_expert, 0))
    step_expert = jnp.where(jnp.repeat(tile_valid, 2) > 0, step_expert, last)
    return step_expert, tile_valid, token_of, gates


def _moe_kernel(se_ref, tv_ref, tok_ref, x_hbm, gates_ref, gain_ref, wg_ref, wu_ref, wd_ref,
                out_hbm, xbuf, x2_scr, hbuf, obuf, sem_in, sem_out):
    del se_ref
    n_tiles = pl.num_programs(0)
    n_tokens = x_hbm.shape[0]
    i = pl.program_id(0)
    s = pl.program_id(1)
    valid = tv_ref[i] > 0
    next_valid = tv_ref[jnp.minimum(i + 1, n_tiles - 1)] * (i + 1 < n_tiles) > 0

    def gather_start(tile):
        def body(r, carry):
            tok = tok_ref[tile * MOE_TILE + r]
            src = jnp.where(tok < n_tokens, tok, 0)
            pltpu.make_async_copy(x_hbm.at[pl.ds(src, 1)], xbuf.at[pl.ds(r, 1)], sem_in).start()
            return carry
        lax.fori_loop(0, MOE_TILE, body, 0, unroll=8)

    def gather_wait():
        pltpu.make_async_copy(x_hbm.at[pl.ds(0, MOE_TILE)], xbuf, sem_in).wait()

    def scatter_start(tile):
        def body(r, carry):
            tok = tok_ref[tile * MOE_TILE + r]
            pltpu.make_async_copy(obuf.at[pl.ds(r, 1)], out_hbm.at[pl.ds(tok, 1)], sem_out).start()
            return carry
        lax.fori_loop(0, MOE_TILE, body, 0, unroll=8)

    def scatter_wait():
        pltpu.make_async_copy(obuf, out_hbm.at[pl.ds(0, MOE_TILE)], sem_out).wait()

    def expert_mlp():
        which = s ^ (i & 1)
        gate = jnp.where(which == 0, gates_ref[:, 0:1], gates_ref[:, 1:2])
        h = hbuf[...]
        hg = jnp.dot(h, wg_ref[...], preferred_element_type=F32)
        hu = jnp.dot(h, wu_ref[...], preferred_element_type=F32)
        act = (jax.nn.silu(hg) * hu * gate).astype(BF16)
        return jnp.dot(act, wd_ref[...], preferred_element_type=F32)

    @pl.when(valid & (s == 0))
    def _():
        @pl.when(i == 0)
        def _():
            gather_start(0)
            obuf[...] = jnp.zeros(obuf.shape, obuf.dtype)

        gather_wait()
        x = _rows_2d(xbuf, x2_scr)
        hbuf[...] = _rms(x, gain_ref[...]).astype(BF16)

        nxt = jnp.minimum(i + 1, n_tiles - 1) * MOE_TILE
        prv = jnp.maximum(i - 1, 0) * MOE_TILE
        for r in range(MOE_TILE):
            tok = tok_ref[nxt + r]
            src = jnp.where(tok < n_tokens, tok, 0)
            pltpu.make_async_copy(x_hbm.at[pl.ds(src, 1)], xbuf.at[pl.ds(r, 1)], sem_in).start()
            dst = jnp.where(i > 0, tok_ref[prv + r], n_tokens + r)
            pltpu.make_async_copy(obuf.at[pl.ds(r, 1)], out_hbm.at[pl.ds(dst, 1)], sem_out).start()

        x2_scr[...] = x2_scr[...] + expert_mlp()

    @pl.when(valid & (s == 1))
    def _():
        @pl.when(jnp.logical_not(next_valid))
        def _():
            gather_wait()

        y = expert_mlp()
        scatter_wait()
        obuf[...] = (x2_scr[...] + y).reshape(obuf.shape)

        @pl.when(jnp.logical_not(next_valid))
        def _():
            scatter_start(i)
            scatter_wait()


def _grouped_moe(x3, route, gain, wg_bf16, wu_bf16, wd_bf16):
    t, _, d = x3.shape
    f = wg_bf16.shape[2]
    n_tiles = t // MOE_TILE + N_BUCKETS
    step_expert, tile_valid, token_of, gates = _moe_plan(route, n_tiles)
    grid_spec = pltpu.PrefetchScalarGridSpec(
        num_scalar_prefetch=3,
        grid=(n_tiles, 2),
        in_specs=[pl.BlockSpec(memory_space=pl.ANY),
                  pl.BlockSpec((MOE_TILE, 2), lambda i, s, se, tv, tok: (i, 0)),
                  pl.BlockSpec((1, d), lambda i, s, se, tv, tok: (0, 0)),
                  pl.BlockSpec((None, d, f), lambda i, s, se, tv, tok: (se[2 * i + s], 0, 0)),
                  pl.BlockSpec((None, d, f), lambda i, s, se, tv, tok: (se[2 * i + s], 0, 0)),
                  pl.BlockSpec((None, f, d), lambda i, s, se, tv, tok: (se[2 * i + s], 0, 0))],
        out_specs=pl.BlockSpec(memory_space=pl.ANY),
        scratch_shapes=[pltpu.VMEM((MOE_TILE, 1, d), F32),
                        pltpu.VMEM((MOE_TILE, d), F32),
                        pltpu.VMEM((MOE_TILE, d), BF16),
                        pltpu.VMEM((MOE_TILE, 1, d), F32),
                        pltpu.SemaphoreType.DMA(()),
                        pltpu.SemaphoreType.DMA(())],
    )
    return pl.pallas_call(
        _moe_kernel,
        grid_spec=grid_spec,
        out_shape=jax.ShapeDtypeStruct((t + MOE_TILE, 1, d), F32),
        compiler_params=_cparams("arbitrary", "arbitrary"),
        name="moe_experts",
    )(step_expert, tile_valid, token_of, x3, gates, gain[None, :], wg_bf16, wu_bf16, wd_bf16)


def kernel(x, positions, norm_mix, norm_ffn, a_w_in, a_q_norm, a_k_norm, b_w_group, b_scale, ab_w_out,
           c_w_in, c_v_norm, c_w_spatial, c_b_spatial, c_w_out, router_w, router_bias,
           expert_w_gate, expert_w_up, expert_w_down):
    batch, t, d = x.shape
    assert batch == 1 and t % (N_STREAM * ATTN_BLOCK) == 0
    depth = norm_mix.shape[0]
    a_width = (a_w_in.shape[2] - b_w_group.shape[1] * b_w_group.shape[2]) // 3
    n_heads = a_width // HEAD_DIM
    ns = t // N_STREAM

    pos_rows = positions[0].reshape(ns, N_STREAM).T.reshape(t)
    cos, sin = (tab.reshape(N_STREAM, ns, HEAD_DIM) for tab in _trig_tables(pos_rows))
    router_w_pad = jnp.pad(router_w, ((0, 0), (0, HEAD_DIM - N_EXPERTS)))
    router_hi = router_w_pad.astype(BF16)
    router_lo = (router_w_pad - router_hi.astype(F32)).astype(BF16)
    a_w_in, b_w_group, ab_w_out, c_w_in, c_w_out = (
        w.astype(BF16) for w in (a_w_in, b_w_group, ab_w_out, c_w_in, c_w_out))
    experts = (expert_w_gate, expert_w_up, expert_w_down)
    n_experts, _, d_expert = expert_w_gate.shape[1:]

    assert depth % 2 == 0

    xs = x[0]
    for layer in range(depth):
        i = layer // 2
        router = (norm_ffn[layer][None, :], router_hi, router_lo, router_bias[:, None])
        if layer % 2 == 0:
            proj3, *w_this = _even_in_proj(xs, t, norm_mix[layer], a_w_in, i,
                                           jnp.stack([a_q_norm[i], a_k_norm[i]]), cos, sin, a_width,
                                           experts, layer)
            oa3, *w_next = _dilated_attention(proj3, n_heads, experts, layer + 1)
            ob3 = _pooling_mixer(proj3, b_w_group, i, b_scale[i])
            x3, route = _even_out_proj(oa3, ob3, ab_w_out, i, xs, t, router)
            wg, wu, wd = w_this
        else:
            z = _odd_in_proj(xs, t, norm_mix[layer], c_w_in, i)
            x3, route = _odd_out_proj(z, c_v_norm[i], c_w_spatial[i], c_b_spatial[i], c_w_out, i, xs, t, router)
            wg, wu, wd = w_next
        xs = _grouped_moe(x3, route, norm_ffn[layer],
                          wg.reshape(n_experts, d, d_expert), wu.reshape(n_experts, d, d_expert),
                          wd.reshape(n_experts, d_expert, d))
    return xs[:t].reshape(1, t, d)
```

```python
import functools

import jax
import jax.numpy as jnp
from jax import lax
from jax.experimental import pallas as pl
from jax.experimental.pallas import tpu as pltpu

F32 = jnp.float32
BF16 = jnp.bfloat16

EPS = 1e-6
HEAD_DIM = 128
ROPE_DIM = HEAD_DIM // 4
ROPE_HALF = ROPE_DIM // 2
ROPE_THETA = 500000.0
POOL_SIZES = (2, 4, 8, 16)
N_STREAM = 16
ATTN_BLOCK = 128
DILATIONS = (16, 4, 1)
GM_GROUPS = 8
GM_CHUNK = 128
N_EXPERTS = 16
N_EXPERT_GROUPS = 4
EXPERTS_PER_GROUP = 4
N_PAIRS = 6
N_BUCKETS = N_EXPERT_GROUPS * N_PAIRS
MOE_TILE = 256
V7X_VMEM_LIMIT = 56 * 1024 * 1024


def _cparams(*sem):
    return pltpu.CompilerParams(dimension_semantics=sem, vmem_limit_bytes=V7X_VMEM_LIMIT)


def _rms(xf, gain_row):
    ms = jnp.mean(xf * xf, axis=-1, keepdims=True)
    return xf * lax.rsqrt(ms + EPS) * gain_row


def _rows_2d(x_ref, x2_scr):
    if len(x_ref.shape) == 2:
        return x_ref[...]
    x2_scr[...] = x_ref[...].reshape(x2_scr.shape)
    return x2_scr[...]


def _cast_specs(weights, layer, n_steps, step_of):
    n_blk = 1
    while n_blk * 2 <= min(n_steps, 128):
        n_blk *= 2
    views, in_specs, out_specs, out_shapes = [], [], [], []
    for w in weights:
        n_layers, e, a, b = w.shape
        rows = e * a // n_blk
        assert rows * n_blk == e * a and rows % 16 == 0
        blk = lambda *g: jnp.minimum(step_of(*g), n_blk - 1)
        views.append(w.reshape(n_layers, e * a, b))
        in_specs.append(pl.BlockSpec((None, rows, b), lambda *g, blk=blk: (layer, blk(*g), 0)))
        out_specs.append(pl.BlockSpec((rows, b), lambda *g, blk=blk: (blk(*g), 0)))
        out_shapes.append(jax.ShapeDtypeStruct((e * a, b), BF16))
    return views, in_specs, out_specs, out_shapes


def _cast_blocks(refs):
    n = len(refs) // 2
    for src, dst in zip(refs[:n], refs[n:]):
        dst[...] = src[...].astype(dst.dtype)


def _x_spec(x, bm, bn, index):
    if x.ndim == 2:
        return pl.BlockSpec((bm, bn), lambda *g: index(*g))
    return pl.BlockSpec((bm, 1, bn), lambda *g: (index(*g)[0], 0, index(*g)[1]))


def _trig_kernel(pos_ref, freq_ref, cos_ref, sin_ref):
    ang = pos_ref[...].astype(F32) * freq_ref[...]
    lane = lax.broadcasted_iota(jnp.int32, ang.shape, 1)
    c = jnp.cos(ang)
    s = jnp.sin(ang)
    cos_ref[...] = jnp.where(lane < ROPE_DIM, c, 1.0)
    sin_ref[...] = jnp.where(lane < ROPE_HALF, -s, jnp.where(lane < ROPE_DIM, s, 0.0))


def _trig_tables(pos_rows):
    t = pos_rows.shape[0]
    inv_freq = ROPE_THETA ** (-jnp.arange(ROPE_HALF, dtype=F32) / ROPE_HALF)
    freq = jnp.tile(inv_freq, HEAD_DIM // ROPE_HALF)[None, :]
    pos_b = jnp.broadcast_to(pos_rows[:, None], (t, HEAD_DIM))
    bm = 1024
    return pl.pallas_call(
        _trig_kernel,
        grid=(t // bm,),
        in_specs=[pl.BlockSpec((bm, HEAD_DIM), lambda i: (i, 0)),
                  pl.BlockSpec((1, HEAD_DIM), lambda i: (0, 0))],
        out_specs=[pl.BlockSpec((bm, HEAD_DIM), lambda i: (i, 0))] * 2,
        out_shape=[jax.ShapeDtypeStruct((t, HEAD_DIM), F32)] * 2,
        compiler_params=_cparams("arbitrary"),
        name="rope_tables",
    )(pos_b, freq)


def _even_in_kernel(x_ref, gain_ref, w_ref, qkg_ref, cos_ref, sin_ref, cg_ref, cu_ref, cd_ref,
                    o_ref, og_ref, ou_ref, od_ref, x2_scr, h_scr, acc_scr, *, n_qk_blocks):
    j = pl.program_id(1)
    n_col = pl.num_programs(1) - 1

    bs = o_ref.shape[1]
    bm = N_STREAM * bs

    def matmul():
        _cast_blocks((cg_ref, cu_ref, cd_ref, og_ref, ou_ref, od_ref))
        acc_scr[j % 2] = jnp.dot(h_scr[...], w_ref[...], preferred_element_type=F32)

    def finish_qk():
        prev = acc_scr.at[(j - 1) % 2]
        is_q = j - 1 < n_qk_blocks // 2
        gain = jnp.where(is_q, qkg_ref[0:1, :], qkg_ref[1:2, :])
        cos = cos_ref[...].reshape(bm, HEAD_DIM) * jnp.where(is_q, HEAD_DIM ** -0.5, 1.0)
        sin = sin_ref[...].reshape(bm, HEAD_DIM) * jnp.where(is_q, HEAD_DIM ** -0.5, 1.0)
        lane = lax.broadcasted_iota(jnp.int32, cos.shape, 1)
        for hh in range(prev.shape[1] // HEAD_DIM):
            y = _rms(prev[:, hh * HEAD_DIM:(hh + 1) * HEAD_DIM], gain)
            swapped = jnp.where(lane < ROPE_HALF,
                                pltpu.roll(y, HEAD_DIM - ROPE_HALF, 1),
                                pltpu.roll(y, ROPE_HALF, 1))
            o_ref[:, :, hh * HEAD_DIM:(hh + 1) * HEAD_DIM] = (y * cos + swapped * sin).reshape(N_STREAM, bs, HEAD_DIM)

    def finish_plain():
        o_ref[...] = acc_scr[(j - 1) % 2].reshape(o_ref.shape)

    @pl.when(j == 0)
    def _():
        i = lax.broadcasted_iota(jnp.int32, (bm, bm), 0)
        c = lax.broadcasted_iota(jnp.int32, (bm, bm), 1)
        perm = jnp.where(c == (i % bs) * N_STREAM + i // bs, 1.0, 0.0).astype(BF16)
        h = _rms(_rows_2d(x_ref, x2_scr), gain_ref[...]).astype(BF16)
        h_scr[...] = jnp.dot(perm, h, preferred_element_type=F32).astype(BF16)
        matmul()

    @pl.when((j >= 1) & (j <= n_qk_blocks))
    def _():
        finish_qk()
        matmul()

    @pl.when((j > n_qk_blocks) & (j < n_col))
    def _():
        finish_plain()
        matmul()

    @pl.when(j == n_col)
    def _():
        _cast_blocks((cg_ref, cu_ref, cd_ref, og_ref, ou_ref, od_ref))
        finish_plain()


def _even_in_proj(x, t, gain, w_bf16, li, qk_gain, cos3, sin3, a_width, experts, layer):
    d = x.shape[-1]
    n = w_bf16.shape[2]
    ns = t // N_STREAM
    bm, bn = 512, 512
    bs = bm // N_STREAM
    n_col = n // bn
    kern = functools.partial(_even_in_kernel, n_qk_blocks=2 * a_width // bn)
    c_views, c_in, c_out, c_shapes = _cast_specs(experts, layer, (t // bm) * (n_col + 1),
                                                 lambda i, j: i * (n_col + 1) + j)
    return pl.pallas_call(
        kern,
        grid=(t // bm, n_col + 1),
        in_specs=[_x_spec(x, bm, d, lambda i, j: (i, 0)),
                  pl.BlockSpec((1, d), lambda i, j: (0, 0)),
                  pl.BlockSpec((None, d, bn), lambda i, j: (li, 0, jnp.minimum(j, n_col - 1))),
                  pl.BlockSpec((2, HEAD_DIM), lambda i, j: (0, 0)),
                  pl.BlockSpec((N_STREAM, bs, HEAD_DIM), lambda i, j: (0, i, 0)),
                  pl.BlockSpec((N_STREAM, bs, HEAD_DIM), lambda i, j: (0, i, 0))] + c_in,
        out_specs=[pl.BlockSpec((N_STREAM, bs, bn), lambda i, j: (0, i, jnp.maximum(j - 1, 0)))] + c_out,
        out_shape=[jax.ShapeDtypeStruct((N_STREAM, ns, n), F32)] + c_shapes,
        scratch_shapes=[pltpu.VMEM((bm, d), F32), pltpu.VMEM((bm, d), BF16), pltpu.VMEM((2, bm, bn), F32)],
        compiler_params=_cparams("arbitrary", "arbitrary"),
        name="even_in_proj",
    )(x, gain[None, :], w_bf16, qk_gain, cos3, sin3, *c_views)


def _attn_kernel(q_ref, k_ref, v_ref, cg_ref, cu_ref, cd_ref, o_ref, og_ref, ou_ref, od_ref, o_scr, l_scr):
    _cast_blocks((cg_ref, cu_ref, cd_ref, og_ref, ou_ref, od_ref))
    sb = pl.program_id(1)
    base = pl.multiple_of(sb * ATTN_BLOCK, ATTN_BLOCK)
    prev_base = pl.multiple_of(jnp.maximum(base - ATTN_BLOCK, 0), ATTN_BLOCK)
    row = lax.broadcasted_iota(jnp.int32, (ATTN_BLOCK, 2 * ATTN_BLOCK), 0)
    col = lax.broadcasted_iota(jnp.int32, (ATTN_BLOCK, 2 * ATTN_BLOCK), 1)
    is_prev = col < ATTN_BLOCK
    colk = col & (ATTN_BLOCK - 1)

    for pat, d in enumerate(DILATIONS):
        c = N_STREAM // d
        cl = ATTN_BLOCK // c
        sh = cl.bit_length() - 1
        qpos = c * (row & (cl - 1)) + (row >> sh)
        kpos = c * (colk & (cl - 1)) + (colk >> sh)
        bias = jnp.where(is_prev,
                         jnp.where(qpos <= kpos, 0.0, -jnp.inf),
                         jnp.where(qpos >= kpos, 0.0, -jnp.inf)).astype(F32)
        bias_start = jnp.where(jnp.logical_and(is_prev, sb == 0), -jnp.inf, bias)

        for r_d in range(d):
            for jj in range(c):
                lo = jj * cl
                rows = [r_d + d * a for a in range(c)]
                if jj == 0:
                    prev = pl.ds(prev_base + (ATTN_BLOCK - cl), cl)
                else:
                    prev = pl.ds(base + (lo - cl), cl)
                cur = pl.ds(base + lo, cl)
                q = jnp.concatenate([q_ref[r, lo:lo + cl, :] for r in rows], axis=0).astype(BF16)
                k = jnp.concatenate([k_ref[r, prev, :] for r in rows] + [k_ref[r, cur, :] for r in rows],
                                    axis=0).astype(BF16)
                v = jnp.concatenate([v_ref[r, prev, :] for r in rows] + [v_ref[r, cur, :] for r in rows],
                                    axis=0).astype(BF16)
                s = lax.dot_general(q, k, (((1,), (1,)), ((), ())), preferred_element_type=F32)
                s = s + (bias_start if jj == 0 else bias)
                m = jnp.max(s, axis=-1, keepdims=True)
                p = jnp.exp(s - m)
                den = jnp.sum(p, axis=-1, keepdims=True)
                o = jnp.dot(p.astype(BF16), v, preferred_element_type=F32) / den
                lse = jnp.broadcast_to(m + jnp.log(den), (ATTN_BLOCK, HEAD_DIM))
                for a, r in enumerate(rows):
                    o_scr[pat, r, lo:lo + cl, :] = o[a * cl:(a + 1) * cl, :]
                    l_scr[pat, r, lo:lo + cl, :] = lse[a * cl:(a + 1) * cl, :]

    for r in range(N_STREAM):
        l0, l1, l2 = l_scr[0, r], l_scr[1, r], l_scr[2, r]
        mx = jnp.maximum(jnp.maximum(l0, l1), l2)
        w0, w1, w2 = jnp.exp(l0 - mx), jnp.exp(l1 - mx), jnp.exp(l2 - mx)
        num = w0 * o_scr[0, r] + w1 * o_scr[1, r] + w2 * o_scr[2, r]
        o_ref[r] = (num / (w0 + w1 + w2)).astype(o_ref.dtype)


def _dilated_attention(proj3, n_heads, experts, cast_layer):
    _, ns, n = proj3.shape
    n_sb = ns // ATTN_BLOCK
    c_views, c_in, c_out, c_shapes = _cast_specs(experts, cast_layer, n_heads * n_sb, lambda h, sb: h * n_sb + sb)
    return pl.pallas_call(
        _attn_kernel,
        grid=(n_heads, n_sb),
        in_specs=[pl.BlockSpec((N_STREAM, ATTN_BLOCK, HEAD_DIM), lambda h, sb: (0, sb, h)),
                  pl.BlockSpec((N_STREAM, ns, HEAD_DIM), lambda h, sb: (0, 0, n_heads + h)),
                  pl.BlockSpec((N_STREAM, ns, HEAD_DIM), lambda h, sb: (0, 0, 2 * n_heads + h))] + c_in,
        out_specs=[pl.BlockSpec((N_STREAM, ATTN_BLOCK, HEAD_DIM), lambda h, sb: (0, sb, h))] + c_out,
        out_shape=[jax.ShapeDtypeStruct((N_STREAM, ns, n_heads * HEAD_DIM), BF16)] + c_shapes,
        scratch_shapes=[pltpu.VMEM((3, N_STREAM, ATTN_BLOCK, HEAD_DIM), F32),
                        pltpu.VMEM((3, N_STREAM, ATTN_BLOCK, HEAD_DIM), F32)],
        compiler_params=_cparams("arbitrary", "arbitrary"),
        name="dilated_attention",
    )(proj3, proj3, proj3, *c_views)


def _pool_kernel(pb_ref, w_ref, sc_ref, o_ref, pre_scr):
    g = pl.program_id(0)
    ns, gw = pb_ref.shape[1], pb_ref.shape[2]
    first = lax.broadcasted_iota(jnp.int32, (ns, gw), 0) == 0

    for r in range(N_STREAM):
        pre_scr[r] = pb_ref[r] if r == 0 else pre_scr[r - 1] + pb_ref[r]

    for gi, p in enumerate(POOL_SIZES):
        @pl.when(g == gi)
        def _(p=p):
            for r in range(N_STREAM):
                win = pre_scr[r] - pre_scr[r - p] if r - p >= 0 else pre_scr[r]
                if r - p + 1 < 0:
                    wrap = pre_scr[N_STREAM - 1] - pre_scr[r - p + N_STREAM]
                    win = win + jnp.where(first, 0.0, pltpu.roll(wrap, 1, 0))
                cnt = jnp.where(first, float(min(r + 1, p)), float(p))
                pooled = win / cnt - pb_ref[r]
                y = jnp.dot(pooled.astype(BF16), w_ref[...], preferred_element_type=F32) * sc_ref[...]
                o_ref[r] = y.astype(o_ref.dtype)


def _pooling_mixer(proj3, w_group_bf16, li, scale):
    _, ns, n = proj3.shape
    _, n_groups, gw, _ = w_group_bf16.shape
    pb_blk0 = (n - n_groups * gw) // gw
    return pl.pallas_call(
        _pool_kernel,
        grid=(n_groups,),
        in_specs=[pl.BlockSpec((N_STREAM, ns, gw), lambda g: (0, 0, pb_blk0 + g)),
                  pl.BlockSpec((None, None, gw, gw), lambda g: (li, g, 0, 0)),
                  pl.BlockSpec((1, gw), lambda g: (0, g))],
        out_specs=pl.BlockSpec((N_STREAM, ns, gw), lambda g: (0, 0, g)),
        out_shape=jax.ShapeDtypeStruct((N_STREAM, ns, n_groups * gw), BF16),
        scratch_shapes=[pltpu.VMEM((N_STREAM, ns, gw), F32)],
        compiler_params=_cparams("arbitrary"),
        name="pooling_mixer",
    )(proj3, w_group_bf16, scale[None, :])


def _even_out_kernel(a_ref, b_ref, wa_ref, wb_ref, x_ref, fg_ref, rhi_ref, rlo_ref, rb_ref,
                     o_ref, route_ref, a_scr, b_scr, xn_scr, lg_scr, ss_scr, *, width):
    j = pl.program_id(1)
    n_col = pl.num_programs(1) - 1
    bs = a_ref.shape[1]
    bm = N_STREAM * bs

    def project():
        slot = xn_scr.at[j % 2]
        mix = (jnp.dot(a_scr[...], wa_ref[...], preferred_element_type=F32)
               + jnp.dot(b_scr[...], wb_ref[...], preferred_element_type=F32))
        xn = _rows_2d(x_ref, slot) + mix
        slot[...] = xn
        o_ref[...] = xn.reshape(o_ref.shape)

    def route_prev(last):
        _router_accumulate(xn_scr[(j - 1) % 2], last, fg_ref, rhi_ref, rlo_ref, rb_ref, route_ref,
                           lg_scr, ss_scr, width)

    @pl.when(j == 0)
    def _():
        n = lax.broadcasted_iota(jnp.int32, (bm, bm), 0)
        c = lax.broadcasted_iota(jnp.int32, (bm, bm), 1)
        perm = jnp.where(c == (n % N_STREAM) * bs + n // N_STREAM, 1.0, 0.0).astype(BF16)
        a = a_ref[...].reshape(bm, a_ref.shape[2])
        b = b_ref[...].reshape(bm, b_ref.shape[2])
        a_scr[...] = jnp.dot(perm, a, preferred_element_type=F32).astype(BF16)
        b_scr[...] = jnp.dot(perm, b, preferred_element_type=F32).astype(BF16)
        lg_scr[...] = jnp.zeros(lg_scr.shape, F32)
        ss_scr[...] = jnp.zeros(ss_scr.shape, F32)
        project()

    @pl.when((j >= 1) & (j < n_col))
    def _():
        route_prev(False)
        project()

    @pl.when(j == n_col)
    def _():
        route_prev(True)


def _even_out_proj(oa3, ob3, w_bf16, li, x, t, router):
    d = x.shape[-1]
    ka, kb = oa3.shape[2], ob3.shape[2]
    assert ka % kb == 0
    bm, bn = 512, 1024
    bs = bm // N_STREAM
    n_col = d // bn
    cur = lambda j: jnp.minimum(j, n_col - 1)
    r_in, r_out, r_scr = _router_specs(bm, bn)
    return pl.pallas_call(
        functools.partial(_even_out_kernel, width=d),
        grid=(t // bm, n_col + 1),
        in_specs=[pl.BlockSpec((N_STREAM, bs, ka), lambda i, j: (0, i, 0)),
                  pl.BlockSpec((N_STREAM, bs, kb), lambda i, j: (0, i, 0)),
                  pl.BlockSpec((None, ka, bn), lambda i, j: (li, 0, cur(j))),
                  pl.BlockSpec((None, kb, bn), lambda i, j: (li, ka // kb, cur(j))),
                  _x_spec(x, bm, bn, lambda i, j: (i, cur(j)))] + r_in,
        out_specs=[pl.BlockSpec((bm, 1, bn), lambda i, j: (i, 0, cur(j))), r_out],
        out_shape=[jax.ShapeDtypeStruct((t, 1, d), F32), jax.ShapeDtypeStruct((8, t), F32)],
        scratch_shapes=[pltpu.VMEM((bm, ka), BF16), pltpu.VMEM((bm, kb), BF16), pltpu.VMEM((2, bm, bn), F32)] + r_scr,
        compiler_params=_cparams("arbitrary", "arbitrary"),
        name="even_out_proj",
    )(oa3, ob3, w_bf16, w_bf16, x, *router)


def _odd_in_kernel(x_ref, gain_ref, w_ref, o_ref, x2_scr, h_scr):
    @pl.when(pl.program_id(1) == 0)
    def _():
        h_scr[...] = _rms(_rows_2d(x_ref, x2_scr), gain_ref[...]).astype(BF16)

    acc = jnp.dot(h_scr[...], w_ref[...], preferred_element_type=F32)
    o_ref[...] = jax.nn.gelu(acc).astype(o_ref.dtype)


def _odd_in_proj(x, t, gain, w_bf16, li):
    d = x.shape[-1]
    n = w_bf16.shape[2]
    bm, bn = 512, 1024
    return pl.pallas_call(
        _odd_in_kernel,
        grid=(t // bm, n // bn),
        in_specs=[_x_spec(x, bm, d, lambda i, j: (i, 0)),
                  pl.BlockSpec((1, d), lambda i, j: (0, 0)),
                  pl.BlockSpec((None, d, bn), lambda i, j: (li, 0, j))],
        out_specs=pl.BlockSpec((bm, bn), lambda i, j: (i, j)),
        out_shape=jax.ShapeDtypeStruct((t, n), BF16),
        scratch_shapes=[pltpu.VMEM((bm, d), F32), pltpu.VMEM((bm, d), BF16)],
        compiler_params=_cparams("arbitrary", "arbitrary"),
        name="odd_in_proj",
    )(x, gain[None, :], w_bf16)


def _odd_out_kernel(u_ref, v_ref, vg_ref, ws_ref, bcol_ref, w_ref, x_ref, fg_ref, rhi_ref, rlo_ref, rb_ref,
                    o_ref, route_ref, g_scr, xn_scr, lg_scr, ss_scr):
    j = pl.program_id(1)
    n_col = pl.num_programs(1) - 1

    def project():
        slot = xn_scr.at[j % 2]
        xn = _rows_2d(x_ref, slot) + jnp.dot(g_scr[...], w_ref[...], preferred_element_type=F32)
        slot[...] = xn
        o_ref[...] = xn.reshape(o_ref.shape)

    def route_prev(last):
        _router_accumulate(xn_scr[(j - 1) % 2], last, fg_ref, rhi_ref, rlo_ref, rb_ref, route_ref,
                           lg_scr, ss_scr, g_scr.shape[1])

    @pl.when(j == 0)
    def _():
        bm, width = g_scr.shape
        gd = width // GM_GROUPS
        ssq = jnp.zeros((bm, 1), F32)
        for g in range(GM_GROUPS):
            vg = v_ref[:, g * gd:(g + 1) * gd].astype(F32)
            ssq = ssq + jnp.sum(vg * vg, axis=-1, keepdims=True)
        inv = lax.rsqrt(ssq / width + EPS)
        causal = (lax.broadcasted_iota(jnp.int32, (GM_CHUNK, GM_CHUNK), 0)
                  >= lax.broadcasted_iota(jnp.int32, (GM_CHUNK, GM_CHUNK), 1))
        wsum = jnp.where(causal, ws_ref[0], 0.0)
        for g in range(1, GM_GROUPS):
            wsum = wsum + jnp.where(causal, ws_ref[g], 0.0)
        wsum = wsum.astype(BF16)
        for g in range(GM_GROUPS):
            bcol = bcol_ref[:, g:g + 1]
            cols = slice(g * gd, (g + 1) * gd)
            for cc in range(bm // GM_CHUNK):
                rows = slice(cc * GM_CHUNK, (cc + 1) * GM_CHUNK)
                vn = (v_ref[rows, cols].astype(F32) * inv[rows, :] * vg_ref[:, cols]).astype(BF16)
                sv = jnp.dot(wsum, vn, preferred_element_type=F32) + bcol
                g_scr[rows, cols] = (u_ref[rows, cols].astype(F32) * sv).astype(BF16)
        lg_scr[...] = jnp.zeros(lg_scr.shape, F32)
        ss_scr[...] = jnp.zeros(ss_scr.shape, F32)
        project()

    @pl.when((j >= 1) & (j < n_col))
    def _():
        route_prev(False)
        project()

    @pl.when(j == n_col)
    def _():
        route_prev(True)


def _odd_out_proj(z, v_gain, w_spatial, b_spatial, w_bf16, li, x, t, router):
    d = x.shape[-1]
    width = z.shape[1] // 2
    assert width == d
    bm, bn = 512, 1024
    n_col = d // bn
    cur = lambda j: jnp.minimum(j, n_col - 1)
    r_in, r_out, r_scr = _router_specs(bm, bn)
    return pl.pallas_call(
        _odd_out_kernel,
        grid=(t // bm, n_col + 1),
        in_specs=[pl.BlockSpec((bm, width), lambda i, j: (i, 0)),
                  pl.BlockSpec((bm, width), lambda i, j: (i, 1)),
                  pl.BlockSpec((1, width), lambda i, j: (0, 0)),
                  pl.BlockSpec((GM_GROUPS, GM_CHUNK, GM_CHUNK), lambda i, j: (0, 0, 0)),
                  pl.BlockSpec((GM_CHUNK, GM_GROUPS), lambda i, j: (0, 0)),
                  pl.BlockSpec((None, width, bn), lambda i, j: (li, 0, cur(j))),
                  _x_spec(x, bm, bn, lambda i, j: (i, cur(j)))] + r_in,
        out_specs=[pl.BlockSpec((bm, 1, bn), lambda i, j: (i, 0, cur(j))), r_out],
        out_shape=[jax.ShapeDtypeStruct((t, 1, d), F32), jax.ShapeDtypeStruct((8, t), F32)],
        scratch_shapes=[pltpu.VMEM((bm, width), BF16), pltpu.VMEM((2, bm, bn), F32)] + r_scr,
        compiler_params=_cparams("arbitrary", "arbitrary"),
        name="odd_out_proj",
    )(z, z, v_gain[None, :], w_spatial, b_spatial.T, w_bf16, x, *router)


def _router_accumulate(xn, last, gain_ref, whi_ref, wlo_ref, bias_ref, route_ref, lg_scr, ss_scr, width):
    xg = xn * gain_ref[...]
    hi = xg.astype(BF16)
    lo = (xg - hi.astype(F32)).astype(BF16)
    part = (jnp.dot(hi, whi_ref[...], preferred_element_type=F32)
            + jnp.dot(lo, whi_ref[...], preferred_element_type=F32)
            + jnp.dot(hi, wlo_ref[...], preferred_element_type=F32))
    lg_scr[...] = lg_scr[...] + part
    ss_scr[...] = ss_scr[...] + jnp.sum(xn * xn, axis=-1, keepdims=True)

    if last:
        _route_rows(lg_scr[...] * lax.rsqrt(ss_scr[...] / width + EPS), bias_ref, route_ref)


def _route_rows(logits, bias_ref, o_ref):
    lt = logits.T[0:N_EXPERTS, :]
    e = jnp.exp(lt - jnp.max(lt, axis=0, keepdims=True))
    scores = e / jnp.sum(e, axis=0, keepdims=True)
    biased = scores + bias_ref[...]
    sc = [scores[i:i + 1, :] for i in range(N_EXPERTS)]
    bi = [biased[i:i + 1, :] for i in range(N_EXPERTS)]

    def top2_sum(a, b, c, d):
        return jnp.maximum(jnp.maximum(jnp.maximum(a + b, a + c), jnp.maximum(a + d, b + c)),
                           jnp.maximum(b + d, c + d))

    grp = [top2_sum(*bi[EXPERTS_PER_GROUP * g:EXPERTS_PER_GROUP * (g + 1)]) for g in range(N_EXPERT_GROUPS)]
    best = jnp.zeros_like(grp[0], dtype=jnp.int32)
    best_score = grp[0]
    for g in range(1, N_EXPERT_GROUPS):
        upd = grp[g] > best_score
        best = jnp.where(upd, g, best)
        best_score = jnp.where(upd, grp[g], best_score)

    def pick(vals, k):
        out = vals[k]
        for g in range(1, N_EXPERT_GROUPS):
            out = jnp.where(best == g, vals[EXPERTS_PER_GROUP * g + k], out)
        return out

    vb = [pick(bi, k) for k in range(EXPERTS_PER_GROUP)]
    vs = [pick(sc, k) for k in range(EXPERTS_PER_GROUP)]
    i1 = jnp.zeros_like(best)
    m1 = vb[0]
    for k in range(1, EXPERTS_PER_GROUP):
        upd = vb[k] > m1
        i1 = jnp.where(upd, k, i1)
        m1 = jnp.where(upd, vb[k], m1)
    i2 = jnp.zeros_like(best)
    m2 = jnp.full_like(m1, -jnp.inf)
    for k in range(EXPERTS_PER_GROUP):
        upd = (i1 != k) & (vb[k] > m2)
        i2 = jnp.where(upd, k, i2)
        m2 = jnp.where(upd, vb[k], m2)

    def take(idx):
        out = vs[0]
        for k in range(1, EXPERTS_PER_GROUP):
            out = jnp.where(idx == k, vs[k], out)
        return out

    g1, g2 = take(i1), take(i2)
    tot = g1 + g2
    g1, g2 = g1 / tot, g2 / tot
    swap = i2 < i1
    lo = jnp.where(swap, i2, i1)
    hi = jnp.where(swap, i1, i2)
    o_ref[0:1, :] = best.astype(F32)
    o_ref[1:2, :] = lo.astype(F32)
    o_ref[2:3, :] = hi.astype(F32)
    o_ref[3:4, :] = jnp.where(swap, g2, g1)
    o_ref[4:5, :] = jnp.where(swap, g1, g2)
    o_ref[5:8, :] = jnp.zeros((3, best.shape[1]), F32)


def _router_specs(bm, bn):
    prev = lambda j: jnp.maximum(j - 1, 0)
    in_specs = [pl.BlockSpec((1, bn), lambda i, j: (0, prev(j))),
                pl.BlockSpec((bn, HEAD_DIM), lambda i, j: (prev(j), 0)),
                pl.BlockSpec((bn, HEAD_DIM), lambda i, j: (prev(j), 0)),
                pl.BlockSpec((N_EXPERTS, 1), lambda i, j: (0, 0))]
    out_spec = pl.BlockSpec((8, bm), lambda i, j: (0, i))
    scratch = [pltpu.VMEM((bm, HEAD_DIM), F32), pltpu.VMEM((bm, 1), F32)]
    return in_specs, out_spec, scratch


def _moe_plan(route, n_tiles):
    t = route.shape[1]
    i32 = jnp.int32
    best, lo, hi = route[0].astype(i32), route[1].astype(i32), route[2].astype(i32)
    pair = lo * (7 - lo) // 2 + (hi - lo - 1)
    bucket = best * N_PAIRS + pair
    onehot = bucket[:, None] == jnp.arange(N_BUCKETS, dtype=i32)[None, :]
    csum = jnp.cumsum(onehot.astype(i32), axis=0)
    counts = csum[-1]
    tiles = 2 * ((counts + 2 * MOE_TILE - 1) // (2 * MOE_TILE))
    tile_end = jnp.cumsum(tiles)
    tile_start = tile_end - tiles
    total = tile_end[-1]
    pos = jnp.sum(jnp.where(onehot, csum - 1 + tile_start[None, :] * MOE_TILE, 0), axis=1)
    packed = jnp.stack([jnp.arange(t, dtype=F32), route[3], route[4]], axis=1)
    prow = jnp.arange(n_tiles * MOE_TILE, dtype=i32)
    spare = (t + ((prow // MOE_TILE) % 2) * MOE_TILE + prow % MOE_TILE).astype(F32)
    init = jnp.stack([spare, jnp.zeros_like(spare), jnp.zeros_like(spare)], axis=1)
    rows = init.at[pos].set(packed, unique_indices=True)
    token_of = rows[:, 0].astype(i32)
    gates = rows[:, 1:3]
    tile_ids = jnp.arange(n_tiles, dtype=i32)
    tile_bucket = jnp.sum(tile_end[None, :] <= jnp.minimum(tile_ids, total - 1)[:, None], axis=1).astype(i32)
    in_bucket = tile_bucket[:, None] == jnp.arange(N_BUCKETS, dtype=i32)[None, :]
    rows_before = (tile_ids - jnp.sum(jnp.where(in_bucket, tile_start[None, :], 0), axis=1)) * MOE_TILE
    bucket_rows = jnp.sum(jnp.where(in_bucket, counts[None, :], 0), axis=1)
    tile_valid = ((tile_ids < total) & (rows_before < bucket_rows)).astype(i32)
    pair_ids = jnp.arange(n_tiles // 2, dtype=i32)
    pair_bucket = tile_bucket[0::2]
    pair_group = pair_bucket // N_PAIRS
    pair_idx = pair_bucket % N_PAIRS
    pair_lo = (pair_idx >= 3).astype(i32) + (pair_idx >= 5).astype(i32)
    pair_hi = pair_idx - pair_lo * (7 - pair_lo) // 2 + pair_lo + 1
    odd = (pair_ids & 1) == 1
    first = pair_group * EXPERTS_PER_GROUP + jnp.where(odd, pair_hi, pair_lo)
    second = pair_group * EXPERTS_PER_GROUP + jnp.where(odd, pair_lo, pair_hi)
    step_expert = jnp.stack([first, first, second, second], axis=1).reshape(-1).astype(i32)
    last = jnp.sum(jnp.where(jnp.arange(2 * n_tiles, dtype=i32) == 2 * total - 1, step_expert, 0))
    step_expert = jnp.where(jnp.repeat(tile_ids[0::2] < total, 4), step_expert, last)
    return step_expert, tile_valid, token_of, gates


def _moe_kernel(se_ref, tv_ref, tok_ref, x_hbm, gates_ref, gain_ref, wg_ref, wu_ref, wd_ref,
                out_hbm, xbuf, x2_scr, hbuf, obuf, sem_in, sem_out):
    del se_ref
    n_pairs = pl.num_programs(0)
    n_tokens = x_hbm.shape[0]
    i = pl.program_id(0)
    s = pl.program_id(1)
    t0 = 2 * i
    t1 = t0 + 1
    nxt0 = jnp.minimum(t0 + 2, 2 * n_pairs - 2)
    valid0 = tv_ref[t0] > 0
    valid1 = tv_ref[t1] > 0
    next_valid = tv_ref[nxt0] * (i + 1 < n_pairs) > 0

    def gather_start(tile):
        def body(r, carry):
            tok = tok_ref[tile * MOE_TILE + r]
            src = jnp.where(tok < n_tokens, tok, 0)
            pltpu.make_async_copy(x_hbm.at[pl.ds(src, 1)], xbuf.at[pl.ds(r, 1)], sem_in).start()
            return carry
        lax.fori_loop(0, MOE_TILE, body, 0, unroll=8)

    def gather_wait():
        pltpu.make_async_copy(x_hbm.at[pl.ds(0, MOE_TILE)], xbuf, sem_in).wait()

    def scatter_start(tile):
        def body(r, carry):
            tok = tok_ref[tile * MOE_TILE + r]
            pltpu.make_async_copy(obuf.at[pl.ds(r, 1)], out_hbm.at[pl.ds(tok, 1)], sem_out).start()
            return carry
        lax.fori_loop(0, MOE_TILE, body, 0, unroll=8)

    def scatter_wait():
        pltpu.make_async_copy(obuf, out_hbm.at[pl.ds(0, MOE_TILE)], sem_out).wait()

    def gather_rows(base):
        for r in range(MOE_TILE):
            tok = tok_ref[base + r]
            src = jnp.where(tok < n_tokens, tok, 0)
            pltpu.make_async_copy(x_hbm.at[pl.ds(src, 1)], xbuf.at[pl.ds(r, 1)], sem_in).start()

    def scatter_rows(base, to_spare):
        for r in range(MOE_TILE):
            dst = jnp.where(to_spare, n_tokens + r, tok_ref[base + r])
            pltpu.make_async_copy(obuf.at[pl.ds(r, 1)], out_hbm.at[pl.ds(dst, 1)], sem_out).start()

    def stage(slot):
        gather_wait()
        x = _rows_2d(xbuf, x2_scr.at[slot])
        hbuf[slot] = _rms(x, gain_ref[...]).astype(BF16)

    def expert_mlp(slot):
        second = s >= 2
        col = (i & 1) ^ second.astype(jnp.int32)
        gate = jnp.where(col == 0, gates_ref[:, 0:1], gates_ref[:, 1:2])
        h = hbuf[slot]
        hg = jnp.dot(h, wg_ref[...], preferred_element_type=F32)
        hu = jnp.dot(h, wu_ref[...], preferred_element_type=F32)
        act = (jax.nn.silu(hg) * hu * gate).astype(BF16)
        return jnp.dot(act, wd_ref[...], preferred_element_type=F32)

    @pl.when(valid0 & (s == 0))
    def _():
        @pl.when(i == 0)
        def _():
            gather_start(0)
            obuf[...] = jnp.zeros(obuf.shape, obuf.dtype)
            spare = pltpu.make_async_copy(obuf, out_hbm.at[pl.ds(n_tokens + MOE_TILE, MOE_TILE)], sem_out)
            spare.start()
            spare.wait()

        stage(0)
        gather_rows(jnp.where(valid1, t1, nxt0) * MOE_TILE)
        scatter_rows(jnp.maximum(t0 - 2, 0) * MOE_TILE, i == 0)
        x2_scr[0] = x2_scr[0] + expert_mlp(0)

    @pl.when(valid1 & (s == 1))
    def _():
        stage(1)
        gather_rows(nxt0 * MOE_TILE)
        x2_scr[1] = x2_scr[1] + expert_mlp(1)

    @pl.when(valid1 & (s == 2))
    def _():
        y = expert_mlp(1)
        scatter_wait()
        obuf[...] = (x2_scr[1] + y).reshape(obuf.shape)

    @pl.when(valid0 & (s == 3))
    def _():
        @pl.when(jnp.logical_not(next_valid))
        def _():
            gather_wait()

        scatter_rows(t1 * MOE_TILE, False)
        y = expert_mlp(0)
        scatter_wait()

        @pl.when(jnp.logical_not(valid1))
        def _():
            scatter_wait()

        obuf[...] = (x2_scr[0] + y).reshape(obuf.shape)

        @pl.when(jnp.logical_not(next_valid))
        def _():
            scatter_start(t0)
            scatter_wait()


def _grouped_moe(x3, route, gain, wg_bf16, wu_bf16, wd_bf16):
    t, _, d = x3.shape
    f = wg_bf16.shape[2]
    n_pairs = t // (2 * MOE_TILE) + N_BUCKETS
    step_expert, tile_valid, token_of, gates = _moe_plan(route, 2 * n_pairs)
    tile_of = lambda i, s: 2 * i + ((s + 1) // 2) % 2
    grid_spec = pltpu.PrefetchScalarGridSpec(
        num_scalar_prefetch=3,
        grid=(n_pairs, 4),
        in_specs=[pl.BlockSpec(memory_space=pl.ANY),
                  pl.BlockSpec((MOE_TILE, 2), lambda i, s, se, tv, tok: (tile_of(i, s), 0)),
                  pl.BlockSpec((1, d), lambda i, s, se, tv, tok: (0, 0)),
                  pl.BlockSpec((None, d, f), lambda i, s, se, tv, tok: (se[4 * i + s], 0, 0)),
                  pl.BlockSpec((None, d, f), lambda i, s, se, tv, tok: (se[4 * i + s], 0, 0)),
                  pl.BlockSpec((None, f, d), lambda i, s, se, tv, tok: (se[4 * i + s], 0, 0))],
        out_specs=pl.BlockSpec(memory_space=pl.ANY),
        scratch_shapes=[pltpu.VMEM((MOE_TILE, 1, d), F32),
                        pltpu.VMEM((2, MOE_TILE, d), F32),
                        pltpu.VMEM((2, MOE_TILE, d), BF16),
                        pltpu.VMEM((MOE_TILE, 1, d), F32),
                        pltpu.SemaphoreType.DMA(()),
                        pltpu.SemaphoreType.DMA(())],
    )
    return pl.pallas_call(
        _moe_kernel,
        grid_spec=grid_spec,
        out_shape=jax.ShapeDtypeStruct((t + 2 * MOE_TILE, 1, d), F32),
        compiler_params=_cparams("arbitrary", "arbitrary"),
        name="moe_experts",
    )(step_expert, tile_valid, token_of, x3, gates, gain[None, :], wg_bf16, wu_bf16, wd_bf16)


def kernel(x, positions, norm_mix, norm_ffn, a_w_in, a_q_norm, a_k_norm, b_w_group, b_scale, ab_w_out,
           c_w_in, c_v_norm, c_w_spatial, c_b_spatial, c_w_out, router_w, router_bias,
           expert_w_gate, expert_w_up, expert_w_down):
    batch, t, d = x.shape
    assert batch == 1 and t % (N_STREAM * ATTN_BLOCK) == 0
    depth = norm_mix.shape[0]
    a_width = (a_w_in.shape[2] - b_w_group.shape[1] * b_w_group.shape[2]) // 3
    n_heads = a_width // HEAD_DIM
    ns = t // N_STREAM

    pos_rows = positions[0].reshape(ns, N_STREAM).T.reshape(t)
    cos, sin = (tab.reshape(N_STREAM, ns, HEAD_DIM) for tab in _trig_tables(pos_rows))
    router_w_pad = jnp.pad(router_w, ((0, 0), (0, HEAD_DIM - N_EXPERTS)))
    router_hi = router_w_pad.astype(BF16)
    router_lo = (router_w_pad - router_hi.astype(F32)).astype(BF16)
    a_w_in, b_w_group, ab_w_out, c_w_in, c_w_out = (
        w.astype(BF16) for w in (a_w_in, b_w_group, ab_w_out, c_w_in, c_w_out))
    experts = (expert_w_gate, expert_w_up, expert_w_down)
    n_experts, _, d_expert = expert_w_gate.shape[1:]

    assert depth % 2 == 0

    xs = x[0]
    for layer in range(depth):
        i = layer // 2
        router = (norm_ffn[layer][None, :], router_hi, router_lo, router_bias[:, None])
        if layer % 2 == 0:
            proj3, *w_this = _even_in_proj(xs, t, norm_mix[layer], a_w_in, i,
                                           jnp.stack([a_q_norm[i], a_k_norm[i]]), cos, sin, a_width,
                                           experts, layer)
            oa3, *w_next = _dilated_attention(proj3, n_heads, experts, layer + 1)
            ob3 = _pooling_mixer(proj3, b_w_group, i, b_scale[i])
            x3, route = _even_out_proj(oa3, ob3, ab_w_out, i, xs, t, router)
            wg, wu, wd = w_this
        else:
            z = _odd_in_proj(xs, t, norm_mix[layer], c_w_in, i)
            x3, route = _odd_out_proj(z, c_v_norm[i], c_w_spatial[i], c_b_spatial[i], c_w_out, i, xs, t, router)
            wg, wu, wd = w_next
        xs = _grouped_moe(x3, route, norm_ffn[layer],
                          wg.reshape(n_experts, d, d_expert), wu.reshape(n_experts, d, d_expert),
                          wd.reshape(n_experts, d_expert, d))
    return xs[:t].reshape(1, t, d)
```

```python
import functools

import jax
import jax.numpy as jnp
from jax import lax
from jax.experimental import pallas as pl
from jax.experimental.pallas import tpu as pltpu

F32 = jnp.float32
BF16 = jnp.bfloat16

EPS = 1e-6
HEAD_DIM = 128
ROPE_DIM = HEAD_DIM // 4
ROPE_HALF = ROPE_DIM // 2
ROPE_THETA = 500000.0
POOL_SIZES = (2, 4, 8, 16)
N_STREAM = 16
ATTN_BLOCK = 128
DILATIONS = (16, 4, 1)
GM_GROUPS = 8
GM_CHUNK = 128
N_EXPERTS = 16
N_EXPERT_GROUPS = 4
EXPERTS_PER_GROUP = 4
N_PAIRS = 6
N_BUCKETS = N_EXPERT_GROUPS * N_PAIRS
MOE_TILE = 256
V7X_VMEM_LIMIT = 56 * 1024 * 1024


def _cparams(*sem):
    return pltpu.CompilerParams(dimension_semantics=sem, vmem_limit_bytes=V7X_VMEM_LIMIT)


def _rms(xf, gain_row):
    ms = jnp.mean(xf * xf, axis=-1, keepdims=True)
    return xf * lax.rsqrt(ms + EPS) * gain_row


def _rows_2d(x_ref, x2_scr):
    if len(x_ref.shape) == 2:
        return x_ref[...]
    x2_scr[...] = x_ref[...].reshape(x2_scr.shape)
    return x2_scr[...]


def _cast_specs(weights, layer, n_steps, step_of):
    n_blk = 1
    while n_blk * 2 <= min(n_steps, 128):
        n_blk *= 2
    views, in_specs, out_specs, out_shapes = [], [], [], []
    for w in weights:
        n_layers, e, a, b = w.shape
        rows = e * a // n_blk
        assert rows * n_blk == e * a and rows % 16 == 0
        blk = lambda *g: jnp.minimum(step_of(*g), n_blk - 1)
        views.append(w.reshape(n_layers, e * a, b))
        in_specs.append(pl.BlockSpec((None, rows, b), lambda *g, blk=blk: (layer, blk(*g), 0)))
        out_specs.append(pl.BlockSpec((rows, b), lambda *g, blk=blk: (blk(*g), 0)))
        out_shapes.append(jax.ShapeDtypeStruct((e * a, b), BF16))
    return views, in_specs, out_specs, out_shapes


def _cast_blocks(refs):
    n = len(refs) // 2
    for src, dst in zip(refs[:n], refs[n:]):
        dst[...] = src[...].astype(dst.dtype)


def _x_spec(x, bm, bn, index):
    if x.ndim == 2:
        return pl.BlockSpec((bm, bn), lambda *g: index(*g))
    return pl.BlockSpec((bm, 1, bn), lambda *g: (index(*g)[0], 0, index(*g)[1]))


def _trig_kernel(pos_ref, freq_ref, cos_ref, sin_ref):
    ang = pos_ref[...].astype(F32) * freq_ref[...]
    lane = lax.broadcasted_iota(jnp.int32, ang.shape, 1)
    c = jnp.cos(ang)
    s = jnp.sin(ang)
    cos_ref[...] = jnp.where(lane < ROPE_DIM, c, 1.0)
    sin_ref[...] = jnp.where(lane < ROPE_HALF, -s, jnp.where(lane < ROPE_DIM, s, 0.0))


def _trig_tables(pos_rows):
    t = pos_rows.shape[0]
    inv_freq = ROPE_THETA ** (-jnp.arange(ROPE_HALF, dtype=F32) / ROPE_HALF)
    freq = jnp.tile(inv_freq, HEAD_DIM // ROPE_HALF)[None, :]
    pos_b = jnp.broadcast_to(pos_rows[:, None], (t, HEAD_DIM))
    bm = 1024
    return pl.pallas_call(
        _trig_kernel,
        grid=(t // bm,),
        in_specs=[pl.BlockSpec((bm, HEAD_DIM), lambda i: (i, 0)),
                  pl.BlockSpec((1, HEAD_DIM), lambda i: (0, 0))],
        out_specs=[pl.BlockSpec((bm, HEAD_DIM), lambda i: (i, 0))] * 2,
        out_shape=[jax.ShapeDtypeStruct((t, HEAD_DIM), F32)] * 2,
        compiler_params=_cparams("arbitrary"),
        name="rope_tables",
    )(pos_b, freq)


def _even_in_kernel(x_ref, gain_ref, w_ref, qkg_ref, cos_ref, sin_ref, cg_ref, cu_ref, cd_ref,
                    o_ref, og_ref, ou_ref, od_ref, x2_scr, h_scr, acc_scr, *, n_qk_blocks):
    j = pl.program_id(1)
    n_col = pl.num_programs(1) - 1

    bs = o_ref.shape[1]
    bm = N_STREAM * bs

    def matmul():
        _cast_blocks((cg_ref, cu_ref, cd_ref, og_ref, ou_ref, od_ref))
        acc_scr[j % 2] = jnp.dot(h_scr[...], w_ref[...], preferred_element_type=F32)

    def finish_qk():
        prev = acc_scr.at[(j - 1) % 2]
        is_q = j - 1 < n_qk_blocks // 2
        gain = jnp.where(is_q, qkg_ref[0:1, :], qkg_ref[1:2, :])
        cos = cos_ref[...].reshape(bm, HEAD_DIM) * jnp.where(is_q, HEAD_DIM ** -0.5, 1.0)
        sin = sin_ref[...].reshape(bm, HEAD_DIM) * jnp.where(is_q, HEAD_DIM ** -0.5, 1.0)
        lane = lax.broadcasted_iota(jnp.int32, cos.shape, 1)
        for hh in range(prev.shape[1] // HEAD_DIM):
            y = _rms(prev[:, hh * HEAD_DIM:(hh + 1) * HEAD_DIM], gain)
            swapped = jnp.where(lane < ROPE_HALF,
                                pltpu.roll(y, HEAD_DIM - ROPE_HALF, 1),
                                pltpu.roll(y, ROPE_HALF, 1))
            o_ref[:, :, hh * HEAD_DIM:(hh + 1) * HEAD_DIM] = (y * cos + swapped * sin).reshape(N_STREAM, bs, HEAD_DIM)

    def finish_plain():
        o_ref[...] = acc_scr[(j - 1) % 2].reshape(o_ref.shape)

    @pl.when(j == 0)
    def _():
        i = lax.broadcasted_iota(jnp.int32, (bm, bm), 0)
        c = lax.broadcasted_iota(jnp.int32, (bm, bm), 1)
        perm = jnp.where(c == (i % bs) * N_STREAM + i // bs, 1.0, 0.0).astype(BF16)
        h = _rms(_rows_2d(x_ref, x2_scr), gain_ref[...]).astype(BF16)
        h_scr[...] = jnp.dot(perm, h, preferred_element_type=F32).astype(BF16)
        matmul()

    @pl.when((j >= 1) & (j <= n_qk_blocks))
    def _():
        finish_qk()
        matmul()

    @pl.when((j > n_qk_blocks) & (j < n_col))
    def _():
        finish_plain()
        matmul()

    @pl.when(j == n_col)
    def _():
        _cast_blocks((cg_ref, cu_ref, cd_ref, og_ref, ou_ref, od_ref))
        finish_plain()


def _even_in_proj(x, t, gain, w_bf16, li, qk_gain, cos3, sin3, a_width, experts, layer):
    d = x.shape[-1]
    n = w_bf16.shape[2]
    ns = t // N_STREAM
    bm, bn = 512, 512
    bs = bm // N_STREAM
    n_col = n // bn
    kern = functools.partial(_even_in_kernel, n_qk_blocks=2 * a_width // bn)
    c_views, c_in, c_out, c_shapes = _cast_specs(experts, layer, (t // bm) * (n_col + 1),
                                                 lambda i, j: i * (n_col + 1) + j)
    return pl.pallas_call(
        kern,
        grid=(t // bm, n_col + 1),
        in_specs=[_x_spec(x, bm, d, lambda i, j: (i, 0)),
                  pl.BlockSpec((1, d), lambda i, j: (0, 0)),
                  pl.BlockSpec((None, d, bn), lambda i, j: (li, 0, jnp.minimum(j, n_col - 1))),
                  pl.BlockSpec((2, HEAD_DIM), lambda i, j: (0, 0)),
                  pl.BlockSpec((N_STREAM, bs, HEAD_DIM), lambda i, j: (0, i, 0)),
                  pl.BlockSpec((N_STREAM, bs, HEAD_DIM), lambda i, j: (0, i, 0))] + c_in,
        out_specs=[pl.BlockSpec((N_STREAM, bs, bn), lambda i, j: (0, i, jnp.maximum(j - 1, 0)))] + c_out,
        out_shape=[jax.ShapeDtypeStruct((N_STREAM, ns, n), F32)] + c_shapes,
        scratch_shapes=[pltpu.VMEM((bm, d), F32), pltpu.VMEM((bm, d), BF16), pltpu.VMEM((2, bm, bn), F32)],
        compiler_params=_cparams("arbitrary", "arbitrary"),
        name="even_in_proj",
    )(x, gain[None, :], w_bf16, qk_gain, cos3, sin3, *c_views)


def _attn_kernel(q_ref, k_ref, v_ref, cg_ref, cu_ref, cd_ref, o_ref, og_ref, ou_ref, od_ref, o_scr, l_scr):
    _cast_blocks((cg_ref, cu_ref, cd_ref, og_ref, ou_ref, od_ref))
    sb = pl.program_id(1)
    base = pl.multiple_of(sb * ATTN_BLOCK, ATTN_BLOCK)
    prev_base = pl.multiple_of(jnp.maximum(base - ATTN_BLOCK, 0), ATTN_BLOCK)
    row = lax.broadcasted_iota(jnp.int32, (ATTN_BLOCK, 2 * ATTN_BLOCK), 0)
    col = lax.broadcasted_iota(jnp.int32, (ATTN_BLOCK, 2 * ATTN_BLOCK), 1)
    is_prev = col < ATTN_BLOCK
    colk = col & (ATTN_BLOCK - 1)

    for pat, d in enumerate(DILATIONS):
        c = N_STREAM // d
        cl = ATTN_BLOCK // c
        sh = cl.bit_length() - 1
        qpos = c * (row & (cl - 1)) + (row >> sh)
        kpos = c * (colk & (cl - 1)) + (colk >> sh)
        bias = jnp.where(is_prev,
                         jnp.where(qpos <= kpos, 0.0, -jnp.inf),
                         jnp.where(qpos >= kpos, 0.0, -jnp.inf)).astype(F32)
        bias_start = jnp.where(jnp.logical_and(is_prev, sb == 0), -jnp.inf, bias)

        for r_d in range(d):
            for jj in range(c):
                lo = jj * cl
                rows = [r_d + d * a for a in range(c)]
                if jj == 0:
                    prev = pl.ds(prev_base + (ATTN_BLOCK - cl), cl)
                else:
                    prev = pl.ds(base + (lo - cl), cl)
                cur = pl.ds(base + lo, cl)
                q = jnp.concatenate([q_ref[r, lo:lo + cl, :] for r in rows], axis=0).astype(BF16)
                k = jnp.concatenate([k_ref[r, prev, :] for r in rows] + [k_ref[r, cur, :] for r in rows],
                                    axis=0).astype(BF16)
                v = jnp.concatenate([v_ref[r, prev, :] for r in rows] + [v_ref[r, cur, :] for r in rows],
                                    axis=0).astype(BF16)
                s = lax.dot_general(q, k, (((1,), (1,)), ((), ())), preferred_element_type=F32)
                s = s + (bias_start if jj == 0 else bias)
                m = jnp.max(s, axis=-1, keepdims=True)
                p = jnp.exp(s - m)
                den = jnp.sum(p, axis=-1, keepdims=True)
                o = jnp.dot(p.astype(BF16), v, preferred_element_type=F32) / den
                lse = jnp.broadcast_to(m + jnp.log(den), (ATTN_BLOCK, HEAD_DIM))
                for a, r in enumerate(rows):
                    o_scr[pat, r, lo:lo + cl, :] = o[a * cl:(a + 1) * cl, :]
                    l_scr[pat, r, lo:lo + cl, :] = lse[a * cl:(a + 1) * cl, :]

    for r in range(N_STREAM):
        l0, l1, l2 = l_scr[0, r], l_scr[1, r], l_scr[2, r]
        mx = jnp.maximum(jnp.maximum(l0, l1), l2)
        w0, w1, w2 = jnp.exp(l0 - mx), jnp.exp(l1 - mx), jnp.exp(l2 - mx)
        num = w0 * o_scr[0, r] + w1 * o_scr[1, r] + w2 * o_scr[2, r]
        o_ref[r] = (num / (w0 + w1 + w2)).astype(o_ref.dtype)


def _dilated_attention(proj3, n_heads, experts, cast_layer):
    _, ns, n = proj3.shape
    n_sb = ns // ATTN_BLOCK
    c_views, c_in, c_out, c_shapes = _cast_specs(experts, cast_layer, n_heads * n_sb, lambda h, sb: h * n_sb + sb)
    return pl.pallas_call(
        _attn_kernel,
        grid=(n_heads, n_sb),
        in_specs=[pl.BlockSpec((N_STREAM, ATTN_BLOCK, HEAD_DIM), lambda h, sb: (0, sb, h)),
                  pl.BlockSpec((N_STREAM, ns, HEAD_DIM), lambda h, sb: (0, 0, n_heads + h)),
                  pl.BlockSpec((N_STREAM, ns, HEAD_DIM), lambda h, sb: (0, 0, 2 * n_heads + h))] + c_in,
        out_specs=[pl.BlockSpec((N_STREAM, ATTN_BLOCK, HEAD_DIM), lambda h, sb: (0, sb, h))] + c_out,
        out_shape=[jax.ShapeDtypeStruct((N_STREAM, ns, n_heads * HEAD_DIM), BF16)] + c_shapes,
        scratch_shapes=[pltpu.VMEM((3, N_STREAM, ATTN_BLOCK, HEAD_DIM), F32),
                        pltpu.VMEM((3, N_STREAM, ATTN_BLOCK, HEAD_DIM), F32)],
        compiler_params=_cparams("arbitrary", "arbitrary"),
        name="dilated_attention",
    )(proj3, proj3, proj3, *c_views)


def _pool_kernel(pb_ref, w_ref, sc_ref, o_ref, pre_scr):
    g = pl.program_id(0)
    ns, gw = pb_ref.shape[1], pb_ref.shape[2]
    first = lax.broadcasted_iota(jnp.int32, (ns, gw), 0) == 0

    for r in range(N_STREAM):
        pre_scr[r] = pb_ref[r] if r == 0 else pre_scr[r - 1] + pb_ref[r]

    for gi, p in enumerate(POOL_SIZES):
        @pl.when(g == gi)
        def _(p=p):
            for r in range(N_STREAM):
                win = pre_scr[r] - pre_scr[r - p] if r - p >= 0 else pre_scr[r]
                if r - p + 1 < 0:
                    wrap = pre_scr[N_STREAM - 1] - pre_scr[r - p + N_STREAM]
                    win = win + jnp.where(first, 0.0, pltpu.roll(wrap, 1, 0))
                cnt = jnp.where(first, float(min(r + 1, p)), float(p))
                pooled = win / cnt - pb_ref[r]
                y = jnp.dot(pooled.astype(BF16), w_ref[...], preferred_element_type=F32) * sc_ref[...]
                o_ref[r] = y.astype(o_ref.dtype)


def _pooling_mixer(proj3, w_group_bf16, li, scale):
    _, ns, n = proj3.shape
    _, n_groups, gw, _ = w_group_bf16.shape
    pb_blk0 = (n - n_groups * gw) // gw
    return pl.pallas_call(
        _pool_kernel,
        grid=(n_groups,),
        in_specs=[pl.BlockSpec((N_STREAM, ns, gw), lambda g: (0, 0, pb_blk0 + g)),
                  pl.BlockSpec((None, None, gw, gw), lambda g: (li, g, 0, 0)),
                  pl.BlockSpec((1, gw), lambda g: (0, g))],
        out_specs=pl.BlockSpec((N_STREAM, ns, gw), lambda g: (0, 0, g)),
        out_shape=jax.ShapeDtypeStruct((N_STREAM, ns, n_groups * gw), BF16),
        scratch_shapes=[pltpu.VMEM((N_STREAM, ns, gw), F32)],
        compiler_params=_cparams("arbitrary"),
        name="pooling_mixer",
    )(proj3, w_group_bf16, scale[None, :])


def _even_out_kernel(a_ref, b_ref, wa_ref, wb_ref, x_ref, fg_ref, rhi_ref, rlo_ref, rb_ref,
                     o_ref, route_ref, a_scr, b_scr, xn_scr, lg_scr, ss_scr, *, width):
    j = pl.program_id(1)
    n_col = pl.num_programs(1) - 1
    bs = a_ref.shape[1]
    bm = N_STREAM * bs

    def project():
        slot = xn_scr.at[j % 2]
        mix = (jnp.dot(a_scr[...], wa_ref[...], preferred_element_type=F32)
               + jnp.dot(b_scr[...], wb_ref[...], preferred_element_type=F32))
        xn = _rows_2d(x_ref, slot) + mix
        slot[...] = xn
        o_ref[...] = xn.reshape(o_ref.shape)

    def route_prev(last):
        _router_accumulate(xn_scr[(j - 1) % 2], last, fg_ref, rhi_ref, rlo_ref, rb_ref, route_ref,
                           lg_scr, ss_scr, width)

    @pl.when(j == 0)
    def _():
        n = lax.broadcasted_iota(jnp.int32, (bm, bm), 0)
        c = lax.broadcasted_iota(jnp.int32, (bm, bm), 1)
        perm = jnp.where(c == (n % N_STREAM) * bs + n // N_STREAM, 1.0, 0.0).astype(BF16)
        a = a_ref[...].reshape(bm, a_ref.shape[2])
        b = b_ref[...].reshape(bm, b_ref.shape[2])
        a_scr[...] = jnp.dot(perm, a, preferred_element_type=F32).astype(BF16)
        b_scr[...] = jnp.dot(perm, b, preferred_element_type=F32).astype(BF16)
        lg_scr[...] = jnp.zeros(lg_scr.shape, F32)
        ss_scr[...] = jnp.zeros(ss_scr.shape, F32)
        project()

    @pl.when((j >= 1) & (j < n_col))
    def _():
        route_prev(False)
        project()

    @pl.when(j == n_col)
    def _():
        route_prev(True)


def _even_out_proj(oa3, ob3, w_bf16, li, x, t, router):
    d = x.shape[-1]
    ka, kb = oa3.shape[2], ob3.shape[2]
    assert ka % kb == 0
    bm, bn = 512, 1024
    bs = bm // N_STREAM
    n_col = d // bn
    cur = lambda j: jnp.minimum(j, n_col - 1)
    r_in, r_out, r_scr = _router_specs(bm, bn)
    return pl.pallas_call(
        functools.partial(_even_out_kernel, width=d),
        grid=(t // bm, n_col + 1),
        in_specs=[pl.BlockSpec((N_STREAM, bs, ka), lambda i, j: (0, i, 0)),
                  pl.BlockSpec((N_STREAM, bs, kb), lambda i, j: (0, i, 0)),
                  pl.BlockSpec((None, ka, bn), lambda i, j: (li, 0, cur(j))),
                  pl.BlockSpec((None, kb, bn), lambda i, j: (li, ka // kb, cur(j))),
                  _x_spec(x, bm, bn, lambda i, j: (i, cur(j)))] + r_in,
        out_specs=[pl.BlockSpec((bm, 1, bn), lambda i, j: (i, 0, cur(j))), r_out],
        out_shape=[jax.ShapeDtypeStruct((t, 1, d), F32), jax.ShapeDtypeStruct((8, t), F32)],
        scratch_shapes=[pltpu.VMEM((bm, ka), BF16), pltpu.VMEM((bm, kb), BF16), pltpu.VMEM((2, bm, bn), F32)] + r_scr,
        compiler_params=_cparams("arbitrary", "arbitrary"),
        name="even_out_proj",
    )(oa3, ob3, w_bf16, w_bf16, x, *router)


def _odd_in_kernel(x_ref, gain_ref, w_ref, o_ref, x2_scr, h_scr):
    @pl.when(pl.program_id(1) == 0)
    def _():
        h_scr[...] = _rms(_rows_2d(x_ref, x2_scr), gain_ref[...]).astype(BF16)

    acc = jnp.dot(h_scr[...], w_ref[...], preferred_element_type=F32)
    o_ref[...] = jax.nn.gelu(acc).astype(o_ref.dtype)


def _odd_in_proj(x, t, gain, w_bf16, li):
    d = x.shape[-1]
    n = w_bf16.shape[2]
    bm, bn = 512, 1024
    return pl.pallas_call(
        _odd_in_kernel,
        grid=(t // bm, n // bn),
        in_specs=[_x_spec(x, bm, d, lambda i, j: (i, 0)),
                  pl.BlockSpec((1, d), lambda i, j: (0, 0)),
                  pl.BlockSpec((None, d, bn), lambda i, j: (li, 0, j))],
        out_specs=pl.BlockSpec((bm, bn), lambda i, j: (i, j)),
        out_shape=jax.ShapeDtypeStruct((t, n), BF16),
        scratch_shapes=[pltpu.VMEM((bm, d), F32), pltpu.VMEM((bm, d), BF16)],
        compiler_params=_cparams("arbitrary", "arbitrary"),
        name="odd_in_proj",
    )(x, gain[None, :], w_bf16)


def _odd_out_kernel(u_ref, v_ref, vg_ref, ws_ref, bcol_ref, w_ref, x_ref, fg_ref, rhi_ref, rlo_ref, rb_ref,
                    o_ref, route_ref, g_scr, xn_scr, lg_scr, ss_scr):
    j = pl.program_id(1)
    n_col = pl.num_programs(1) - 1

    def project():
        slot = xn_scr.at[j % 2]
        xn = _rows_2d(x_ref, slot) + jnp.dot(g_scr[...], w_ref[...], preferred_element_type=F32)
        slot[...] = xn
        o_ref[...] = xn.reshape(o_ref.shape)

    def route_prev(last):
        _router_accumulate(xn_scr[(j - 1) % 2], last, fg_ref, rhi_ref, rlo_ref, rb_ref, route_ref,
                           lg_scr, ss_scr, g_scr.shape[1])

    @pl.when(j == 0)
    def _():
        bm, width = g_scr.shape
        gd = width // GM_GROUPS
        ssq = jnp.zeros((bm, 1), F32)
        for g in range(GM_GROUPS):
            vg = v_ref[:, g * gd:(g + 1) * gd].astype(F32)
            ssq = ssq + jnp.sum(vg * vg, axis=-1, keepdims=True)
        inv = lax.rsqrt(ssq / width + EPS)
        causal = (lax.broadcasted_iota(jnp.int32, (GM_CHUNK, GM_CHUNK), 0)
                  >= lax.broadcasted_iota(jnp.int32, (GM_CHUNK, GM_CHUNK), 1))
        wsum = jnp.where(causal, ws_ref[0], 0.0)
        for g in range(1, GM_GROUPS):
            wsum = wsum + jnp.where(causal, ws_ref[g], 0.0)
        wsum = wsum.astype(BF16)
        for g in range(GM_GROUPS):
            bcol = bcol_ref[:, g:g + 1]
            cols = slice(g * gd, (g + 1) * gd)
            for cc in range(bm // GM_CHUNK):
                rows = slice(cc * GM_CHUNK, (cc + 1) * GM_CHUNK)
                vn = (v_ref[rows, cols].astype(F32) * inv[rows, :] * vg_ref[:, cols]).astype(BF16)
                sv = jnp.dot(wsum, vn, preferred_element_type=F32) + bcol
                g_scr[rows, cols] = (u_ref[rows, cols].astype(F32) * sv).astype(BF16)
        lg_scr[...] = jnp.zeros(lg_scr.shape, F32)
        ss_scr[...] = jnp.zeros(ss_scr.shape, F32)
        project()

    @pl.when((j >= 1) & (j < n_col))
    def _():
        route_prev(False)
        project()

    @pl.when(j == n_col)
    def _():
        route_prev(True)


def _odd_out_proj(z, v_gain, w_spatial, b_spatial, w_bf16, li, x, t, router):
    d = x.shape[-1]
    width = z.shape[1] // 2
    assert width == d
    bm, bn = 512, 1024
    n_col = d // bn
    cur = lambda j: jnp.minimum(j, n_col - 1)
    r_in, r_out, r_scr = _router_specs(bm, bn)
    return pl.pallas_call(
        _odd_out_kernel,
        grid=(t // bm, n_col + 1),
        in_specs=[pl.BlockSpec((bm, width), lambda i, j: (i, 0)),
                  pl.BlockSpec((bm, width), lambda i, j: (i, 1)),
                  pl.BlockSpec((1, width), lambda i, j: (0, 0)),
                  pl.BlockSpec((GM_GROUPS, GM_CHUNK, GM_CHUNK), lambda i, j: (0, 0, 0)),
                  pl.BlockSpec((GM_CHUNK, GM_GROUPS), lambda i, j: (0, 0)),
                  pl.BlockSpec((None, width, bn), lambda i, j: (li, 0, cur(j))),
                  _x_spec(x, bm, bn, lambda i, j: (i, cur(j)))] + r_in,
        out_specs=[pl.BlockSpec((bm, 1, bn), lambda i, j: (i, 0, cur(j))), r_out],
        out_shape=[jax.ShapeDtypeStruct((t, 1, d), F32), jax.ShapeDtypeStruct((8, t), F32)],
        scratch_shapes=[pltpu.VMEM((bm, width), BF16), pltpu.VMEM((2, bm, bn), F32)] + r_scr,
        compiler_params=_cparams("arbitrary", "arbitrary"),
        name="odd_out_proj",
    )(z, z, v_gain[None, :], w_spatial, b_spatial.T, w_bf16, x, *router)


def _router_accumulate(xn, last, gain_ref, whi_ref, wlo_ref, bias_ref, route_ref, lg_scr, ss_scr, width):
    xg = xn * gain_ref[...]
    hi = xg.astype(BF16)
    lo = (xg - hi.astype(F32)).astype(BF16)
    part = (jnp.dot(hi, whi_ref[...], preferred_element_type=F32)
            + jnp.dot(lo, whi_ref[...], preferred_element_type=F32)
            + jnp.dot(hi, wlo_ref[...], preferred_element_type=F32))
    lg_scr[...] = lg_scr[...] + part
    ss_scr[...] = ss_scr[...] + jnp.sum(xn * xn, axis=-1, keepdims=True)

    if last:
        _route_rows(lg_scr[...] * lax.rsqrt(ss_scr[...] / width + EPS), bias_ref, route_ref)


def _route_rows(logits, bias_ref, o_ref):
    lt = logits.T[0:N_EXPERTS, :]
    e = jnp.exp(lt - jnp.max(lt, axis=0, keepdims=True))
    scores = e / jnp.sum(e, axis=0, keepdims=True)
    biased = scores + bias_ref[...]
    sc = [scores[i:i + 1, :] for i in range(N_EXPERTS)]
    bi = [biased[i:i + 1, :] for i in range(N_EXPERTS)]

    def top2_sum(a, b, c, d):
        return jnp.maximum(jnp.maximum(jnp.maximum(a + b, a + c), jnp.maximum(a + d, b + c)),
                           jnp.maximum(b + d, c + d))

    grp = [top2_sum(*bi[EXPERTS_PER_GROUP * g:EXPERTS_PER_GROUP * (g + 1)]) for g in range(N_EXPERT_GROUPS)]
    best = jnp.zeros_like(grp[0], dtype=jnp.int32)
    best_score = grp[0]
    for g in range(1, N_EXPERT_GROUPS):
        upd = grp[g] > best_score
        best = jnp.where(upd, g, best)
        best_score = jnp.where(upd, grp[g], best_score)

    def pick(vals, k):
        out = vals[k]
        for g in range(1, N_EXPERT_GROUPS):
            out = jnp.where(best == g, vals[EXPERTS_PER_GROUP * g + k], out)
        return out

    vb = [pick(bi, k) for k in range(EXPERTS_PER_GROUP)]
    vs = [pick(sc, k) for k in range(EXPERTS_PER_GROUP)]
    i1 = jnp.zeros_like(best)
    m1 = vb[0]
    for k in range(1, EXPERTS_PER_GROUP):
        upd = vb[k] > m1
        i1 = jnp.where(upd, k, i1)
        m1 = jnp.where(upd, vb[k], m1)
    i2 = jnp.zeros_like(best)
    m2 = jnp.full_like(m1, -jnp.inf)
    for k in range(EXPERTS_PER_GROUP):
        upd = (i1 != k) & (vb[k] > m2)
        i2 = jnp.where(upd, k, i2)
        m2 = jnp.where(upd, vb[k], m2)

    def take(idx):
        out = vs[0]
        for k in range(1, EXPERTS_PER_GROUP):
            out = jnp.where(idx == k, vs[k], out)
        return out

    g1, g2 = take(i1), take(i2)
    tot = g1 + g2
    g1, g2 = g1 / tot, g2 / tot
    swap = i2 < i1
    lo = jnp.where(swap, i2, i1)
    hi = jnp.where(swap, i1, i2)
    o_ref[0:1, :] = best.astype(F32)
    o_ref[1:2, :] = lo.astype(F32)
    o_ref[2:3, :] = hi.astype(F32)
    o_ref[3:4, :] = jnp.where(swap, g2, g1)
    o_ref[4:5, :] = jnp.where(swap, g1, g2)
    o_ref[5:8, :] = jnp.zeros((3, best.shape[1]), F32)


def _router_specs(bm, bn):
    prev = lambda j: jnp.maximum(j - 1, 0)
    in_specs = [pl.BlockSpec((1, bn), lambda i, j: (0, prev(j))),
                pl.BlockSpec((bn, HEAD_DIM), lambda i, j: (prev(j), 0)),
                pl.BlockSpec((bn, HEAD_DIM), lambda i, j: (prev(j), 0)),
                pl.BlockSpec((N_EXPERTS, 1), lambda i, j: (0, 0))]
    out_spec = pl.BlockSpec((8, bm), lambda i, j: (0, i))
    scratch = [pltpu.VMEM((bm, HEAD_DIM), F32), pltpu.VMEM((bm, 1), F32)]
    return in_specs, out_spec, scratch


def _moe_plan(route, n_tiles):
    t = route.shape[1]
    i32 = jnp.int32
    best, lo, hi = route[0].astype(i32), route[1].astype(i32), route[2].astype(i32)
    pair = lo * (7 - lo) // 2 + (hi - lo - 1)
    bucket = best * N_PAIRS + pair
    onehot = bucket[:, None] == jnp.arange(N_BUCKETS, dtype=i32)[None, :]
    csum = jnp.cumsum(onehot.astype(i32), axis=0)
    counts = csum[-1]
    tiles = (counts + MOE_TILE - 1) // MOE_TILE
    tile_end = jnp.cumsum(tiles)
    tile_start = tile_end - tiles
    total = tile_end[-1]
    pos = jnp.sum(jnp.where(onehot, csum - 1 + tile_start[None, :] * MOE_TILE, 0), axis=1)
    packed = jnp.stack([jnp.arange(t, dtype=F32), route[3], route[4]], axis=1)
    prow = jnp.arange(n_tiles * MOE_TILE, dtype=i32)
    spare = (t + ((prow // MOE_TILE) % 2) * MOE_TILE + prow % MOE_TILE).astype(F32)
    init = jnp.stack([spare, jnp.zeros_like(spare), jnp.zeros_like(spare)], axis=1)
    rows = init.at[pos].set(packed, unique_indices=True)
    token_of = rows[:, 0].astype(i32)
    gates = rows[:, 1:3]
    tile_ids = jnp.arange(n_tiles, dtype=i32)
    tile_valid = (tile_ids < total).astype(i32)
    tile_bucket = jnp.sum(tile_end[None, :] <= jnp.minimum(tile_ids, total - 1)[:, None], axis=1).astype(i32)
    tile_group = tile_bucket // N_PAIRS
    tile_pair = tile_bucket % N_PAIRS
    pair_lo = (tile_pair >= 3).astype(i32) + (tile_pair >= 5).astype(i32)
    pair_hi = tile_pair - pair_lo * (7 - pair_lo) // 2 + pair_lo + 1
    odd = (tile_ids & 1) == 1
    first = tile_group * EXPERTS_PER_GROUP + jnp.where(odd, pair_hi, pair_lo)
    second = tile_group * EXPERTS_PER_GROUP + jnp.where(odd, pair_lo, pair_hi)
    step_expert = jnp.stack([first, second], axis=1).reshape(-1).astype(i32)
    last = jnp.sum(jnp.where(jnp.arange(2 * n_tiles, dtype=i32) == 2 * total - 1, step_expert, 0))
    step_expert = jnp.where(jnp.repeat(tile_valid, 2) > 0, step_expert, last)
    return step_expert, tile_valid, token_of, gates


def _moe_kernel(se_ref, tv_ref, tok_ref, x_hbm, gates_ref, gain_ref, wg_ref, wu_ref, wd_ref,
                out_hbm, xbuf, x2_scr, hbuf, obuf, sem_in, sem_out):
    del se_ref
    n_tiles = pl.num_programs(0)
    n_tokens = x_hbm.shape[0]
    i = pl.program_id(0)
    s = pl.program_id(1)
    valid = tv_ref[i] > 0
    next_valid = tv_ref[jnp.minimum(i + 1, n_tiles - 1)] * (i + 1 < n_tiles) > 0
    cur = i % 2
    oth = 1 - cur

    def gather_copy(slot, r, tok):
        src = jnp.where(tok < n_tokens, tok, 0)
        return pltpu.make_async_copy(x_hbm.at[pl.ds(src, 1)], xbuf.at[slot, pl.ds(r, 1)], sem_in.at[slot])

    def scatter_copy(slot, r, dst):
        return pltpu.make_async_copy(obuf.at[slot, pl.ds(r, 1)], out_hbm.at[pl.ds(dst, 1)], sem_out.at[slot])

    def gather_loop(slot, tile):
        def body(r, carry):
            gather_copy(slot, r, tok_ref[tile * MOE_TILE + r]).start()
            return carry
        lax.fori_loop(0, MOE_TILE, body, 0, unroll=8)

    def scatter_loop(slot, tile):
        def body(r, carry):
            scatter_copy(slot, r, tok_ref[tile * MOE_TILE + r]).start()
            return carry
        lax.fori_loop(0, MOE_TILE, body, 0, unroll=8)

    def gather_wait(slot):
        pltpu.make_async_copy(x_hbm.at[pl.ds(0, MOE_TILE)], xbuf.at[slot], sem_in.at[slot]).wait()

    def scatter_wait(slot):
        pltpu.make_async_copy(obuf.at[slot], out_hbm.at[pl.ds(0, MOE_TILE)], sem_out.at[slot]).wait()

    def expert_mlp():
        which = s ^ (i & 1)
        gate = jnp.where(which == 0, gates_ref[:, 0:1], gates_ref[:, 1:2])
        h = hbuf[...]
        hg = jnp.dot(h, wg_ref[...], preferred_element_type=F32)
        hu = jnp.dot(h, wu_ref[...], preferred_element_type=F32)
        act = (jax.nn.silu(hg) * hu * gate).astype(BF16)
        return jnp.dot(act, wd_ref[...], preferred_element_type=F32)

    @pl.when(valid & (s == 0))
    def _():
        @pl.when(i == 0)
        def _():
            gather_loop(0, 0)
            gather_loop(1, jnp.minimum(1, n_tiles - 1))
            obuf[1] = jnp.zeros(obuf.shape[1:], obuf.dtype)
            spare = pltpu.make_async_copy(obuf.at[1], out_hbm.at[pl.ds(n_tokens + MOE_TILE, MOE_TILE)],
                                          sem_out.at[1])
            spare.start()
            spare.wait()

        gather_wait(cur)
        x = _rows_2d(xbuf.at[cur], x2_scr)
        hbuf[...] = _rms(x, gain_ref[...]).astype(BF16)

        nxt = jnp.minimum(i + 2, n_tiles - 1) * MOE_TILE
        prv = jnp.maximum(i - 1, 0) * MOE_TILE
        for r in range(MOE_TILE):
            gather_copy(cur, r, tok_ref[nxt + r]).start()
            scatter_copy(oth, r, jnp.where(i > 0, tok_ref[prv + r], n_tokens + r)).start()

        x2_scr[...] = x2_scr[...] + expert_mlp()

    @pl.when(valid & (s == 1))
    def _():
        @pl.when(jnp.logical_not(next_valid))
        def _():
            gather_wait(0)
            gather_wait(1)

        y = expert_mlp()

        @pl.when(i > 0)
        def _():
            scatter_wait(cur)

        obuf[cur] = (x2_scr[...] + y).reshape(obuf.shape[1:])

        @pl.when(jnp.logical_not(next_valid))
        def _():
            scatter_wait(oth)
            scatter_loop(cur, i)
            scatter_wait(cur)


def _grouped_moe(x3, route, gain, wg_bf16, wu_bf16, wd_bf16):
    t, _, d = x3.shape
    f = wg_bf16.shape[2]
    n_tiles = t // MOE_TILE + N_BUCKETS
    step_expert, tile_valid, token_of, gates = _moe_plan(route, n_tiles)
    grid_spec = pltpu.PrefetchScalarGridSpec(
        num_scalar_prefetch=3,
        grid=(n_tiles, 2),
        in_specs=[pl.BlockSpec(memory_space=pl.ANY),
                  pl.BlockSpec((MOE_TILE, 2), lambda i, s, se, tv, tok: (i, 0)),
                  pl.BlockSpec((1, d), lambda i, s, se, tv, tok: (0, 0)),
                  pl.BlockSpec((None, d, f), lambda i, s, se, tv, tok: (se[2 * i + s], 0, 0)),
                  pl.BlockSpec((None, d, f), lambda i, s, se, tv, tok: (se[2 * i + s], 0, 0)),
                  pl.BlockSpec((None, f, d), lambda i, s, se, tv, tok: (se[2 * i + s], 0, 0))],
        out_specs=pl.BlockSpec(memory_space=pl.ANY),
        scratch_shapes=[pltpu.VMEM((2, MOE_TILE, 1, d), F32),
                        pltpu.VMEM((MOE_TILE, d), F32),
                        pltpu.VMEM((MOE_TILE, d), BF16),
                        pltpu.VMEM((2, MOE_TILE, 1, d), F32),
                        pltpu.SemaphoreType.DMA((2,)),
                        pltpu.SemaphoreType.DMA((2,))],
    )
    return pl.pallas_call(
        _moe_kernel,
        grid_spec=grid_spec,
        out_shape=jax.ShapeDtypeStruct((t + 2 * MOE_TILE, 1, d), F32),
        compiler_params=_cparams("arbitrary", "arbitrary"),
        name="moe_experts",
    )(step_expert, tile_valid, token_of, x3, gates, gain[None, :], wg_bf16, wu_bf16, wd_bf16)


def kernel(x, positions, norm_mix, norm_ffn, a_w_in, a_q_norm, a_k_norm, b_w_group, b_scale, ab_w_out,
           c_w_in, c_v_norm, c_w_spatial, c_b_spatial, c_w_out, router_w, router_bias,
           expert_w_gate, expert_w_up, expert_w_down):
    batch, t, d = x.shape
    assert batch == 1 and t % (N_STREAM * ATTN_BLOCK) == 0
    depth = norm_mix.shape[0]
    a_width = (a_w_in.shape[2] - b_w_group.shape[1] * b_w_group.shape[2]) // 3
    n_heads = a_width // HEAD_DIM
    ns = t // N_STREAM

    pos_rows = positions[0].reshape(ns, N_STREAM).T.reshape(t)
    cos, sin = (tab.reshape(N_STREAM, ns, HEAD_DIM) for tab in _trig_tables(pos_rows))
    router_w_pad = jnp.pad(router_w, ((0, 0), (0, HEAD_DIM - N_EXPERTS)))
    router_hi = router_w_pad.astype(BF16)
    router_lo = (router_w_pad - router_hi.astype(F32)).astype(BF16)
    a_w_in, b_w_group, ab_w_out, c_w_in, c_w_out = (
        w.astype(BF16) for w in (a_w_in, b_w_group, ab_w_out, c_w_in, c_w_out))
    experts = (expert_w_gate, expert_w_up, expert_w_down)
    n_experts, _, d_expert = expert_w_gate.shape[1:]

    assert depth % 2 == 0

    xs = x[0]
    for layer in range(depth):
        i = layer // 2
        router = (norm_ffn[layer][None, :], router_hi, router_lo, router_bias[:, None])
        if layer % 2 == 0:
            proj3, *w_this = _even_in_proj(xs, t, norm_mix[layer], a_w_in, i,
                                           jnp.stack([a_q_norm[i], a_k_norm[i]]), cos, sin, a_width,
                                           experts, layer)
            oa3, *w_next = _dilated_attention(proj3, n_heads, experts, layer + 1)
            ob3 = _pooling_mixer(proj3, b_w_group, i, b_scale[i])
            x3, route = _even_out_proj(oa3, ob3, ab_w_out, i, xs, t, router)
            wg, wu, wd = w_this
        else:
            z = _odd_in_proj(xs, t, norm_mix[layer], c_w_in, i)
            x3, route = _odd_out_proj(z, c_v_norm[i], c_w_spatial[i], c_b_spatial[i], c_w_out, i, xs, t, router)
            wg, wu, wd = w_next
        xs = _grouped_moe(x3, route, norm_ffn[layer],
                          wg.reshape(n_experts, d, d_expert), wu.reshape(n_experts, d, d_expert),
                          wd.reshape(n_experts, d_expert, d))
    return xs[:t].reshape(1, t, d)
```

```python
import functools

import jax
import jax.numpy as jnp
from jax import lax
from jax.experimental import pallas as pl
from jax.experimental.pallas import tpu as pltpu

F32 = jnp.float32
BF16 = jnp.bfloat16

EPS = 1e-6
HEAD_DIM = 128
ROPE_DIM = HEAD_DIM // 4
ROPE_HALF = ROPE_DIM // 2
ROPE_THETA = 500000.0
POOL_SIZES = (2, 4, 8, 16)
N_STREAM = 16
ATTN_BLOCK = 128
DILATIONS = (16, 4, 1)
GM_GROUPS = 8
GM_CHUNK = 128
N_EXPERTS = 16
N_EXPERT_GROUPS = 4
EXPERTS_PER_GROUP = 4
N_PAIRS = 6
N_BUCKETS = N_EXPERT_GROUPS * N_PAIRS
MOE_TILE = 256
V7X_VMEM_LIMIT = 56 * 1024 * 1024


def _cparams(*sem):
    return pltpu.CompilerParams(dimension_semantics=sem, vmem_limit_bytes=V7X_VMEM_LIMIT)


def _rms(xf, gain_row):
    ms = jnp.mean(xf * xf, axis=-1, keepdims=True)
    return xf * lax.rsqrt(ms + EPS) * gain_row


def _rows_2d(x_ref, x2_scr):
    if len(x_ref.shape) == 2:
        return x_ref[...]
    x2_scr[...] = x_ref[...].reshape(x2_scr.shape)
    return x2_scr[...]


def _cast_specs(weights, layer, n_steps, step_of):
    n_blk = 1
    while n_blk * 2 <= min(n_steps, 128):
        n_blk *= 2
    views, in_specs, out_specs, out_shapes = [], [], [], []
    for w in weights:
        n_layers, e, a, b = w.shape
        rows = e * a // n_blk
        assert rows * n_blk == e * a and rows % 16 == 0
        blk = lambda *g: jnp.minimum(step_of(*g), n_blk - 1)
        views.append(w.reshape(n_layers, e * a, b))
        in_specs.append(pl.BlockSpec((None, rows, b), lambda *g, blk=blk: (layer, blk(*g), 0)))
        out_specs.append(pl.BlockSpec((rows, b), lambda *g, blk=blk: (blk(*g), 0)))
        out_shapes.append(jax.ShapeDtypeStruct((e * a, b), BF16))
    return views, in_specs, out_specs, out_shapes


def _cast_blocks(refs):
    n = len(refs) // 2
    for src, dst in zip(refs[:n], refs[n:]):
        dst[...] = src[...].astype(dst.dtype)


def _x_spec(x, bm, bn, index):
    if x.ndim == 2:
        return pl.BlockSpec((bm, bn), lambda *g: index(*g))
    return pl.BlockSpec((bm, 1, bn), lambda *g: (index(*g)[0], 0, index(*g)[1]))


def _trig_kernel(pos_ref, freq_ref, cos_ref, sin_ref):
    ang = pos_ref[...].astype(F32) * freq_ref[...]
    lane = lax.broadcasted_iota(jnp.int32, ang.shape, 1)
    c = jnp.cos(ang)
    s = jnp.sin(ang)
    cos_ref[...] = jnp.where(lane < ROPE_DIM, c, 1.0)
    sin_ref[...] = jnp.where(lane < ROPE_HALF, -s, jnp.where(lane < ROPE_DIM, s, 0.0))


def _trig_tables(pos_rows):
    t = pos_rows.shape[0]
    inv_freq = ROPE_THETA ** (-jnp.arange(ROPE_HALF, dtype=F32) / ROPE_HALF)
    freq = jnp.tile(inv_freq, HEAD_DIM // ROPE_HALF)[None, :]
    pos_b = jnp.broadcast_to(pos_rows[:, None], (t, HEAD_DIM))
    bm = 1024
    return pl.pallas_call(
        _trig_kernel,
        grid=(t // bm,),
        in_specs=[pl.BlockSpec((bm, HEAD_DIM), lambda i: (i, 0)),
                  pl.BlockSpec((1, HEAD_DIM), lambda i: (0, 0))],
        out_specs=[pl.BlockSpec((bm, HEAD_DIM), lambda i: (i, 0))] * 2,
        out_shape=[jax.ShapeDtypeStruct((t, HEAD_DIM), F32)] * 2,
        compiler_params=_cparams("arbitrary"),
        name="rope_tables",
    )(pos_b, freq)


def _even_in_kernel(x_ref, gain_ref, w_ref, qkg_ref, cos_ref, sin_ref, cg_ref, cu_ref, cd_ref,
                    o_ref, og_ref, ou_ref, od_ref, x2_scr, h_scr, acc_scr, *, n_qk_blocks):
    j = pl.program_id(1)
    n_col = pl.num_programs(1) - 1

    bs = o_ref.shape[1]
    bm = N_STREAM * bs

    def matmul():
        _cast_blocks((cg_ref, cu_ref, cd_ref, og_ref, ou_ref, od_ref))
        acc_scr[j % 2] = jnp.dot(h_scr[...], w_ref[...], preferred_element_type=F32)

    def finish_qk():
        prev = acc_scr.at[(j - 1) % 2]
        is_q = j - 1 < n_qk_blocks // 2
        gain = jnp.where(is_q, qkg_ref[0:1, :], qkg_ref[1:2, :])
        cos = cos_ref[...].reshape(bm, HEAD_DIM) * jnp.where(is_q, HEAD_DIM ** -0.5, 1.0)
        sin = sin_ref[...].reshape(bm, HEAD_DIM) * jnp.where(is_q, HEAD_DIM ** -0.5, 1.0)
        lane = lax.broadcasted_iota(jnp.int32, cos.shape, 1)
        for hh in range(prev.shape[1] // HEAD_DIM):
            y = _rms(prev[:, hh * HEAD_DIM:(hh + 1) * HEAD_DIM], gain)
            swapped = jnp.where(lane < ROPE_HALF,
                                pltpu.roll(y, HEAD_DIM - ROPE_HALF, 1),
                                pltpu.roll(y, ROPE_HALF, 1))
            o_ref[:, :, hh * HEAD_DIM:(hh + 1) * HEAD_DIM] = (y * cos + swapped * sin).reshape(N_STREAM, bs, HEAD_DIM)

    def finish_plain():
        o_ref[...] = acc_scr[(j - 1) % 2].reshape(o_ref.shape)

    @pl.when(j == 0)
    def _():
        i = lax.broadcasted_iota(jnp.int32, (bm, bm), 0)
        c = lax.broadcasted_iota(jnp.int32, (bm, bm), 1)
        perm = jnp.where(c == (i % bs) * N_STREAM + i // bs, 1.0, 0.0).astype(BF16)
        h = _rms(_rows_2d(x_ref, x2_scr), gain_ref[...]).astype(BF16)
        h_scr[...] = jnp.dot(perm, h, preferred_element_type=F32).astype(BF16)
        matmul()

    @pl.when((j >= 1) & (j <= n_qk_blocks))
    def _():
        finish_qk()
        matmul()

    @pl.when((j > n_qk_blocks) & (j < n_col))
    def _():
        finish_plain()
        matmul()

    @pl.when(j == n_col)
    def _():
        _cast_blocks((cg_ref, cu_ref, cd_ref, og_ref, ou_ref, od_ref))
        finish_plain()


def _even_in_proj(x, t, gain, w_bf16, li, qk_gain, cos3, sin3, a_width, experts, layer):
    d = x.shape[-1]
    n = w_bf16.shape[2]
    ns = t // N_STREAM
    bm, bn = 512, 512
    bs = bm // N_STREAM
    n_col = n // bn
    kern = functools.partial(_even_in_kernel, n_qk_blocks=2 * a_width // bn)
    c_views, c_in, c_out, c_shapes = _cast_specs(experts, layer, (t // bm) * (n_col + 1),
                                                 lambda i, j: i * (n_col + 1) + j)
    return pl.pallas_call(
        kern,
        grid=(t // bm, n_col + 1),
        in_specs=[_x_spec(x, bm, d, lambda i, j: (i, 0)),
                  pl.BlockSpec((1, d), lambda i, j: (0, 0)),
                  pl.BlockSpec((None, d, bn), lambda i, j: (li, 0, jnp.minimum(j, n_col - 1))),
                  pl.BlockSpec((2, HEAD_DIM), lambda i, j: (0, 0)),
                  pl.BlockSpec((N_STREAM, bs, HEAD_DIM), lambda i, j: (0, i, 0)),
                  pl.BlockSpec((N_STREAM, bs, HEAD_DIM), lambda i, j: (0, i, 0))] + c_in,
        out_specs=[pl.BlockSpec((N_STREAM, bs, bn), lambda i, j: (0, i, jnp.maximum(j - 1, 0)))] + c_out,
        out_shape=[jax.ShapeDtypeStruct((N_STREAM, ns, n), F32)] + c_shapes,
        scratch_shapes=[pltpu.VMEM((bm, d), F32), pltpu.VMEM((bm, d), BF16), pltpu.VMEM((2, bm, bn), F32)],
        compiler_params=_cparams("arbitrary", "arbitrary"),
        name="even_in_proj",
    )(x, gain[None, :], w_bf16, qk_gain, cos3, sin3, *c_views)


def _attn_kernel(q_ref, k_ref, v_ref, cg_ref, cu_ref, cd_ref, o_ref, og_ref, ou_ref, od_ref, o_scr, l_scr):
    _cast_blocks((cg_ref, cu_ref, cd_ref, og_ref, ou_ref, od_ref))
    sb = pl.program_id(1)
    base =pl.multiple_of(sb * ATTN_BLOCK, ATTN_BLOCK)
    prev_base = pl.multiple_of(jnp.maximum(base - ATTN_BLOCK, 0), ATTN_BLOCK)
    row = lax.broadcasted_iota(jnp.int32, (ATTN_BLOCK, 2 * ATTN_BLOCK), 0)
    col = lax.broadcasted_iota(jnp.int32, (ATTN_BLOCK, 2 * ATTN_BLOCK), 1)
    is_prev = col < ATTN_BLOCK
    colk = col & (ATTN_BLOCK - 1)

    for pat, d in enumerate(DILATIONS):
        c = N_STREAM // d
        cl = ATTN_BLOCK // c
        sh = cl.bit_length() - 1
        qpos = c * (row & (cl - 1)) + (row >> sh)
        kpos = c * (colk & (cl - 1)) + (colk >> sh)
        bias = jnp.where(is_prev,
                         jnp.where(qpos <= kpos, 0.0, -jnp.inf),
                         jnp.where(qpos >= kpos, 0.0, -jnp.inf)).astype(F32)
        bias_start = jnp.where(jnp.logical_and(is_prev, sb == 0), -jnp.inf, bias)

        for r_d in range(d):
            for jj in range(c):
                lo = jj * cl
                rows = [r_d + d * a for a in range(c)]
                if jj == 0:
                    prev = pl.ds(prev_base + (ATTN_BLOCK - cl), cl)
                else:
                    prev = pl.ds(base + (lo - cl), cl)
                cur = pl.ds(base + lo, cl)
                q = jnp.concatenate([q_ref[r, lo:lo + cl, :] for r in rows], axis=0).astype(BF16)
                k = jnp.concatenate([k_ref[r, prev, :] for r in rows] + [k_ref[r, cur, :] for r in rows],
                                    axis=0).astype(BF16)
                v = jnp.concatenate([v_ref[r, prev, :] for r in rows] + [v_ref[r, cur, :] for r in rows],
                                    axis=0).astype(BF16)
                s = lax.dot_general(q, k, (((1,), (1,)), ((), ())), preferred_element_type=F32)
                s = s + (bias_start if jj == 0 else bias)
                m = jnp.max(s, axis=-1, keepdims=True)
                p = jnp.exp(s - m)
                den = jnp.sum(p, axis=-1, keepdims=True)
                o = jnp.dot(p.astype(BF16), v, preferred_element_type=F32) / den
                lse = jnp.broadcast_to(m + jnp.log(den), (ATTN_BLOCK, HEAD_DIM))
                for a, r in enumerate(rows):
                    o_scr[pat, r, lo:lo + cl, :] = o[a * cl:(a + 1) * cl, :]
                    l_scr[pat, r, lo:lo + cl, :] = lse[a * cl:(a + 1) * cl, :]

    for r in range(N_STREAM):
        l0, l1, l2 = l_scr[0, r], l_scr[1, r], l_scr[2, r]
        mx = jnp.maximum(jnp.maximum(l0, l1), l2)
        w0, w1, w2 = jnp.exp(l0 - mx), jnp.exp(l1 - mx), jnp.exp(l2 - mx)
        num = w0 * o_scr[0, r] + w1 * o_scr[1, r] + w2 * o_scr[2, r]
        o_ref[r] = (num / (w0 + w1 + w2)).astype(o_ref.dtype)


def _dilated_attention(proj3, n_heads, experts, cast_layer):
    _, ns, n = proj3.shape
    n_sb = ns // ATTN_BLOCK
    c_views, c_in, c_out, c_shapes = _cast_specs(experts, cast_layer, n_heads * n_sb, lambda h, sb: h * n_sb + sb)
    return pl.pallas_call(
        _attn_kernel,
        grid=(n_heads, n_sb),
        in_specs=[pl.BlockSpec((N_STREAM, ATTN_BLOCK, HEAD_DIM), lambda h, sb: (0, sb, h)),
                  pl.BlockSpec((N_STREAM, ns, HEAD_DIM), lambda h, sb: (0, 0, n_heads + h)),
                  pl.BlockSpec((N_STREAM, ns, HEAD_DIM), lambda h, sb: (0, 0, 2 * n_heads + h))] + c_in,
        out_specs=[pl.BlockSpec((N_STREAM, ATTN_BLOCK, HEAD_DIM), lambda h, sb: (0, sb, h))] + c_out,
        out_shape=[jax.ShapeDtypeStruct((N_STREAM, ns, n_heads * HEAD_DIM), BF16)] + c_shapes,
        scratch_shapes=[pltpu.VMEM((3, N_STREAM, ATTN_BLOCK, HEAD_DIM), F32),
                        pltpu.VMEM((3, N_STREAM, ATTN_BLOCK, HEAD_DIM), F32)],
        compiler_params=_cparams("arbitrary", "arbitrary"),
        name="dilated_attention",
    )(proj3, proj3, proj3, *c_views)


def _pool_kernel(pb_ref, w_ref, sc_ref, o_ref, pre_scr):
    g = pl.program_id(0)
    ns, gw = pb_ref.shape[1], pb_ref.shape[2]
    first = lax.broadcasted_iota(jnp.int32, (ns, gw), 0) == 0

    for r in range(N_STREAM):
        pre_scr[r] = pb_ref[r] if r == 0 else pre_scr[r - 1] + pb_ref[r]

    for gi, p in enumerate(POOL_SIZES):
        @pl.when(g == gi)
        def _(p=p):
            for r in range(N_STREAM):
                win = pre_scr[r] - pre_scr[r - p] if r - p >= 0 else pre_scr[r]
                if r - p + 1 < 0:
                    wrap = pre_scr[N_STREAM - 1] - pre_scr[r - p + N_STREAM]
                    win = win + jnp.where(first, 0.0, pltpu.roll(wrap, 1, 0))
                cnt = jnp.where(first, float(min(r + 1, p)), float(p))
                pooled = win / cnt - pb_ref[r]
                y = jnp.dot(pooled.astype(BF16), w_ref[...], preferred_element_type=F32) * sc_ref[...]
                o_ref[r] = y.astype(o_ref.dtype)


def _pooling_mixer(proj3, w_group_bf16, li, scale):
    _, ns, n = proj3.shape
    _, n_groups, gw, _ = w_group_bf16.shape
    pb_blk0 = (n - n_groups * gw) // gw
    return pl.pallas_call(
        _pool_kernel,
        grid=(n_groups,),
        in_specs=[pl.BlockSpec((N_STREAM, ns, gw), lambda g: (0, 0, pb_blk0 + g)),
                  pl.BlockSpec((None, None, gw, gw), lambda g: (li, g, 0, 0)),
                  pl.BlockSpec((1, gw), lambda g: (0, g))],
        out_specs=pl.BlockSpec((N_STREAM, ns, gw), lambda g: (0, 0, g)),
        out_shape=jax.ShapeDtypeStruct((N_STREAM, ns, n_groups * gw), BF16),
        scratch_shapes=[pltpu.VMEM((N_STREAM, ns, gw), F32)],
        compiler_params=_cparams("arbitrary"),
        name="pooling_mixer",
    )(proj3, w_group_bf16, scale[None, :])


def _even_out_kernel(a_ref, b_ref, wa_ref, wb_ref, x_ref, fg_ref, rhi_ref, rlo_ref, rb_ref,
                     o_ref, route_ref, a_scr, b_scr, xn_scr, lg_scr, ss_scr, *, width):
    j = pl.program_id(1)
    n_col = pl.num_programs(1) - 1
    bs = a_ref.shape[1]
    bm = N_STREAM * bs

    def project():
        slot = xn_scr.at[j % 2]
        mix = (jnp.dot(a_scr[...], wa_ref[...], preferred_element_type=F32)
               + jnp.dot(b_scr[...], wb_ref[...], preferred_element_type=F32))
        xn = _rows_2d(x_ref, slot) + mix
        slot[...] = xn
        o_ref[...] = xn.reshape(o_ref.shape)

    def route_prev(last):
        _router_accumulate(xn_scr[(j - 1) % 2], last, fg_ref, rhi_ref, rlo_ref, rb_ref, route_ref,
                           lg_scr, ss_scr, width)

    @pl.when(j == 0)
    def _():
        n = lax.broadcasted_iota(jnp.int32, (bm, bm), 0)
        c = lax.broadcasted_iota(jnp.int32, (bm, bm), 1)
        perm = jnp.where(c == (n % N_STREAM) * bs + n // N_STREAM, 1.0, 0.0).astype(BF16)
        a = a_ref[...].reshape(bm, a_ref.shape[2])
        b = b_ref[...].reshape(bm, b_ref.shape[2])
        a_scr[...] = jnp.dot(perm, a, preferred_element_type=F32).astype(BF16)
        b_scr[...] = jnp.dot(perm, b, preferred_element_type=F32).astype(BF16)
        lg_scr[...] = jnp.zeros(lg_scr.shape, F32)
        ss_scr[...] = jnp.zeros(ss_scr.shape, F32)
        project()

    @pl.when((j >= 1) & (j < n_col))
    def _():
        route_prev(False)
        project()

    @pl.when(j == n_col)
    def _():
        route_prev(True)


def _even_out_proj(oa3, ob3, w_bf16, li, x, t, router):
    d = x.shape[-1]
    ka, kb = oa3.shape[2], ob3.shape[2]
    assert ka % kb == 0
    bm, bn = 512, 1024
    bs = bm // N_STREAM
    n_col = d // bn
    cur = lambda j: jnp.minimum(j, n_col - 1)
    r_in, r_out, r_scr = _router_specs(bm, bn)
    return pl.pallas_call(
        functools.partial(_even_out_kernel, width=d),
        grid=(t // bm, n_col + 1),
        in_specs=[pl.BlockSpec((N_STREAM, bs, ka), lambda i, j: (0, i, 0)),
                  pl.BlockSpec((N_STREAM, bs, kb), lambda i, j: (0, i, 0)),
                  pl.BlockSpec((None, ka, bn), lambda i, j: (li, 0, cur(j))),
                  pl.BlockSpec((None, kb, bn), lambda i, j: (li, ka // kb, cur(j))),
                  _x_spec(x, bm, bn, lambda i, j: (i, cur(j)))] + r_in,
        out_specs=[pl.BlockSpec((bm, 1, bn), lambda i, j: (i, 0, cur(j))), r_out],
        out_shape=[jax.ShapeDtypeStruct((t, 1, d), F32), jax.ShapeDtypeStruct((8, t), F32)],
        scratch_shapes=[pltpu.VMEM((bm, ka), BF16), pltpu.VMEM((bm, kb), BF16), pltpu.VMEM((2, bm, bn), F32)] + r_scr,
        compiler_params=_cparams("arbitrary", "arbitrary"),
        name="even_out_proj",
    )(oa3, ob3, w_bf16, w_bf16, x, *router)


def _odd_in_kernel(x_ref, gain_ref, w_ref, o_ref, x2_scr, h_scr):
    @pl.when(pl.program_id(1) == 0)
    def _():
        h_scr[...] = _rms(_rows_2d(x_ref, x2_scr), gain_ref[...]).astype(BF16)

    acc = jnp.dot(h_scr[...], w_ref[...], preferred_element_type=F32)
    o_ref[...] = jax.nn.gelu(acc).astype(o_ref.dtype)


def _odd_in_proj(x, t, gain, w_bf16, li):
    d = x.shape[-1]
    n = w_bf16.shape[2]
    bm, bn = 512, 1024
    return pl.pallas_call(
        _odd_in_kernel,
        grid=(t // bm, n // bn),
        in_specs=[_x_spec(x, bm, d, lambda i, j: (i, 0)),
                  pl.BlockSpec((1, d), lambda i, j: (0, 0)),
                  pl.BlockSpec((None, d, bn), lambda i, j: (li, 0, j))],
        out_specs=pl.BlockSpec((bm, bn), lambda i, j: (i, j)),
        out_shape=jax.ShapeDtypeStruct((t, n), BF16),
        scratch_shapes=[pltpu.VMEM((bm, d), F32), pltpu.VMEM((bm, d), BF16)],
        compiler_params=_cparams("arbitrary", "arbitrary"),
        name="odd_in_proj",
    )(x, gain[None, :], w_bf16)


def _odd_out_kernel(u_ref, v_ref, vg_ref, ws_ref, bcol_ref, w_ref, x_ref, fg_ref, rhi_ref, rlo_ref, rb_ref,
                    o_ref, route_ref, g_scr, xn_scr, lg_scr, ss_scr):
    j = pl.program_id(1)
    n_col = pl.num_programs(1) - 1

    def project():
        slot = xn_scr.at[j % 2]
        xn = _rows_2d(x_ref, slot) + jnp.dot(g_scr[...], w_ref[...], preferred_element_type=F32)
        slot[...] = xn
        o_ref[...] = xn.reshape(o_ref.shape)

    def route_prev(last):
        _router_accumulate(xn_scr[(j - 1) % 2], last, fg_ref, rhi_ref, rlo_ref, rb_ref, route_ref,
                           lg_scr, ss_scr, g_scr.shape[1])

    @pl.when(j == 0)
    def _():
        bm, width = g_scr.shape
        gd = width // GM_GROUPS
        ssq = jnp.zeros((bm, 1), F32)
        for g in range(GM_GROUPS):
            vg = v_ref[:, g * gd:(g + 1) * gd].astype(F32)
            ssq = ssq + jnp.sum(vg * vg, axis=-1, keepdims=True)
        inv = lax.rsqrt(ssq / width + EPS)
        causal = (lax.broadcasted_iota(jnp.int32, (GM_CHUNK, GM_CHUNK), 0)
                  >= lax.broadcasted_iota(jnp.int32, (GM_CHUNK, GM_CHUNK), 1))
        wsum = jnp.where(causal, ws_ref[0], 0.0)
        for g in range(1, GM_GROUPS):
            wsum = wsum + jnp.where(causal, ws_ref[g], 0.0)
        wsum = wsum.astype(BF16)
        for g in range(GM_GROUPS):
            bcol = bcol_ref[:, g:g + 1]
            cols = slice(g * gd, (g + 1) * gd)
            for cc in range(bm // GM_CHUNK):
                rows = slice(cc * GM_CHUNK, (cc + 1) * GM_CHUNK)
                vn = (v_ref[rows, cols].astype(F32) * inv[rows, :] * vg_ref[:, cols]).astype(BF16)
                sv = jnp.dot(wsum, vn, preferred_element_type=F32) + bcol
                g_scr[rows, cols] = (u_ref[rows, cols].astype(F32) * sv).astype(BF16)
        lg_scr[...] = jnp.zeros(lg_scr.shape, F32)
        ss_scr[...] = jnp.zeros(ss_scr.shape, F32)
        project()

    @pl.when((j >= 1) & (j < n_col))
    def _():
        route_prev(False)
        project()

    @pl.when(j == n_col)
    def _():
        route_prev(True)


def _odd_out_proj(z, v_gain, w_spatial, b_spatial, w_bf16, li, x, t, router):
    d = x.shape[-1]
    width = z.shape[1] // 2
    assert width == d
    bm, bn = 512, 1024
    n_col = d // bn
    cur = lambda j: jnp.minimum(j, n_col - 1)
    r_in, r_out, r_scr = _router_specs(bm, bn)
    return pl.pallas_call(
        _odd_out_kernel,
        grid=(t // bm, n_col + 1),
        in_specs=[pl.BlockSpec((bm, width), lambda i, j: (i, 0)),
                  pl.BlockSpec((bm, width), lambda i, j: (i, 1)),
                  pl.BlockSpec((1, width), lambda i, j: (0, 0)),
                  pl.BlockSpec((GM_GROUPS, GM_CHUNK, GM_CHUNK), lambda i, j: (0, 0, 0)),
                  pl.BlockSpec((GM_CHUNK, GM_GROUPS), lambda i, j: (0, 0)),
                  pl.BlockSpec((None, width, bn), lambda i, j: (li, 0, cur(j))),
                  _x_spec(x, bm, bn, lambda i, j: (i, cur(j)))] + r_in,
        out_specs=[pl.BlockSpec((bm, 1, bn), lambda i, j: (i, 0, cur(j))), r_out],
        out_shape=[jax.ShapeDtypeStruct((t, 1, d), F32), jax.ShapeDtypeStruct((8, t), F32)],
        scratch_shapes=[pltpu.VMEM((bm, width), BF16), pltpu.VMEM((2, bm, bn), F32)] + r_scr,
        compiler_params=_cparams("arbitrary", "arbitrary"),
        name="odd_out_proj",
    )(z, z, v_gain[None, :], w_spatial, b_spatial.T, w_bf16, x, *router)


def _router_accumulate(xn, last, gain_ref, whi_ref, wlo_ref, bias_ref, route_ref, lg_scr, ss_scr, width):
    xg = xn * gain_ref[...]
    hi = xg.astype(BF16)
    lo = (xg - hi.astype(F32)).astype(BF16)
    part = (jnp.dot(hi, whi_ref[...], preferred_element_type=F32)
            + jnp.dot(lo, whi_ref[...], preferred_element_type=F32)
            + jnp.dot(hi, wlo_ref[...], preferred_element_type=F32))
    lg_scr[...] = lg_scr[...] + part
    ss_scr[...] = ss_scr[...] + jnp.sum(xn * xn, axis=-1, keepdims=True)

    if last:
        _route_rows(lg_scr[...] * lax.rsqrt(ss_scr[...] / width + EPS), bias_ref, route_ref)


def _route_rows(logits, bias_ref, o_ref):
    lt = logits.T[0:N_EXPERTS, :]
    e = jnp.exp(lt - jnp.max(lt, axis=0, keepdims=True))
    scores = e / jnp.sum(e, axis=0, keepdims=True)
    biased = scores + bias_ref[...]
    sc = [scores[i:i + 1, :] for i in range(N_EXPERTS)]
    bi = [biased[i:i + 1, :] for i in range(N_EXPERTS)]

    def top2_sum(a, b, c, d):
        return jnp.maximum(jnp.maximum(jnp.maximum(a + b, a + c), jnp.maximum(a + d, b + c)),
                           jnp.maximum(b + d, c + d))

    grp = [top2_sum(*bi[EXPERTS_PER_GROUP * g:EXPERTS_PER_GROUP * (g + 1)]) for g in range(N_EXPERT_GROUPS)]
    best = jnp.zeros_like(grp[0], dtype=jnp.int32)
    best_score = grp[0]
    for g in range(1, N_EXPERT_GROUPS):
        upd = grp[g] > best_score
        best = jnp.where(upd, g, best)
        best_score = jnp.where(upd, grp[g], best_score)

    def pick(vals, k):
        out = vals[k]
        for g in range(1, N_EXPERT_GROUPS):
            out = jnp.where(best == g, vals[EXPERTS_PER_GROUP * g + k], out)
        return out

    vb = [pick(bi, k) for k in range(EXPERTS_PER_GROUP)]
    vs = [pick(sc, k) for k in range(EXPERTS_PER_GROUP)]
    i1 = jnp.zeros_like(best)
    m1 = vb[0]
    for k in range(1, EXPERTS_PER_GROUP):
        upd = vb[k] > m1
        i1 = jnp.where(upd, k, i1)
        m1 = jnp.where(upd, vb[k], m1)
    i2 = jnp.zeros_like(best)
    m2 = jnp.full_like(m1, -jnp.inf)
    for k in range(EXPERTS_PER_GROUP):
        upd = (i1 != k) & (vb[k] > m2)
        i2 = jnp.where(upd, k, i2)
        m2 = jnp.where(upd, vb[k], m2)

    def take(idx):
        out = vs[0]
        for k in range(1, EXPERTS_PER_GROUP):
            out = jnp.where(idx == k, vs[k], out)
        return out

    g1, g2 = take(i1), take(i2)
    tot = g1 + g2
    g1, g2 = g1 / tot, g2 / tot
    swap = i2 < i1
    lo = jnp.where(swap, i2, i1)
    hi = jnp.where(swap, i1, i2)
    o_ref[0:1, :] = best.astype(F32)
    o_ref[1:2, :] = lo.astype(F32)
    o_ref[2:3, :] = hi.astype(F32)
    o_ref[3:4, :] = jnp.where(swap, g2, g1)
    o_ref[4:5, :] = jnp.where(swap, g1, g2)
    o_ref[5:8, :] = jnp.zeros((3, best.shape[1]), F32)


def _router_specs(bm, bn):
    prev = lambda j: jnp.maximum(j - 1, 0)
    in_specs = [pl.BlockSpec((1, bn), lambda i, j: (0, prev(j))),
                pl.BlockSpec((bn, HEAD_DIM), lambda i, j: (prev(j), 0)),
                pl.BlockSpec((bn, HEAD_DIM), lambda i, j: (prev(j), 0)),
                pl.BlockSpec((N_EXPERTS, 1), lambda i, j: (0, 0))]
    out_spec = pl.BlockSpec((8, bm), lambda i, j: (0, i))
    scratch = [pltpu.VMEM((bm, HEAD_DIM), F32), pltpu.VMEM((bm, 1), F32)]
    return in_specs, out_spec, scratch


def _moe_plan(route, n_tiles):
    t = route.shape[1]
    i32 = jnp.int32
    best, lo, hi = route[0].astype(i32), route[1].astype(i32), route[2].astype(i32)
    pair = lo * (7 - lo) // 2 + (hi - lo - 1)
    bucket = best * N_PAIRS + pair
    onehot = bucket[:, None] == jnp.arange(N_BUCKETS, dtype=i32)[None, :]
    csum = jnp.cumsum(onehot.astype(i32), axis=0)
    counts = csum[-1]
    tiles = (counts + MOE_TILE - 1) // MOE_TILE
    tile_end = jnp.cumsum(tiles)
    tile_start = tile_end - tiles
    total = tile_end[-1]
    pos = jnp.sum(jnp.where(onehot, csum - 1 + tile_start[None, :] * MOE_TILE, 0), axis=1)
    packed = jnp.stack([jnp.arange(t, dtype=F32), route[3], route[4]], axis=1)
    spare = (t + jnp.arange(n_tiles * MOE_TILE, dtype=i32) % MOE_TILE).astype(F32)
    init = jnp.stack([spare, jnp.zeros_like(spare), jnp.zeros_like(spare)], axis=1)
    rows = init.at[pos].set(packed, unique_indices=True)
    token_of = rows[:, 0].astype(i32)
    gates = rows[:, 1:3]
    tile_ids = jnp.arange(n_tiles, dtype=i32)
    tile_valid = (tile_ids < total).astype(i32)
    tile_bucket = jnp.sum(tile_end[None, :] <= jnp.minimum(tile_ids, total - 1)[:, None], axis=1).astype(i32)
    tile_group = tile_bucket // N_PAIRS
    tile_pair = tile_bucket % N_PAIRS
    pair_lo = (tile_pair >= 3).astype(i32) + (tile_pair >= 5).astype(i32)
    pair_hi = tile_pair - pair_lo * (7 - pair_lo) // 2 + pair_lo + 1
    odd = (tile_ids & 1) == 1
    first = tile_group * EXPERTS_PER_GROUP + jnp.where(odd, pair_hi, pair_lo)
    second = tile_group * EXPERTS_PER_GROUP + jnp.where(odd, pair_lo, pair_hi)
    step_expert = jnp.stack([first, second], axis=1).reshape(-1).astype(i32)
    last = jnp.sum(jnp.where(jnp.arange(2 * n_tiles, dtype=i32) == 2 * total - 1, step_expert, 0))
    step_expert = jnp.where(jnp.repeat(tile_valid, 2) > 0, step_expert, last)
    return step_expert, tile_valid, token_of, gates


def _moe_kernel(se_ref, tv_ref, tok_ref, x_hbm, gates_ref, gain_ref, wg_ref, wu_ref, wd_ref,
                out_hbm, xbuf, x2_scr, hbuf, obuf, sem_in, sem_out):
    del se_ref
    n_tiles = pl.num_programs(0)
    n_tokens = x_hbm.shape[0]
    i = pl.program_id(0)
    s = pl.program_id(1)
    valid = tv_ref[i] > 0
    next_valid = tv_ref[jnp.minimum(i + 1, n_tiles - 1)] * (i + 1 < n_tiles) > 0

    def gather_start(tile):
        def body(r, carry):
            tok = tok_ref[tile * MOE_TILE + r]
            src = jnp.where(tok < n_tokens, tok, 0)
            pltpu.make_async_copy(x_hbm.at[pl.ds(src, 1)], xbuf.at[pl.ds(r, 1)], sem_in).start()
            return carry
        lax.fori_loop(0, MOE_TILE, body, 0, unroll=8)

    def gather_wait():
        pltpu.make_async_copy(x_hbm.at[pl.ds(0, MOE_TILE)], xbuf, sem_in).wait()

    def scatter_start(tile):
        def body(r, carry):
            tok = tok_ref[tile * MOE_TILE + r]
            pltpu.make_async_copy(obuf.at[pl.ds(r, 1)], out_hbm.at[pl.ds(tok, 1)], sem_out).start()
            return carry
        lax.fori_loop(0, MOE_TILE, body, 0, unroll=8)

    def scatter_wait():
        pltpu.make_async_copy(obuf, out_hbm.at[pl.ds(0, MOE_TILE)], sem_out).wait()

    def expert_mlp():
        which = s ^ (i & 1)
        gate = jnp.where(which == 0, gates_ref[:, 0:1], gates_ref[:, 1:2])
        h = hbuf[...]
        hg = jnp.dot(h, wg_ref[...], preferred_element_type=F32)
        hu = jnp.dot(h, wu_ref[...], preferred_element_type=F32)
        act = (jax.nn.silu(hg) * hu * gate).astype(BF16)
        return jnp.dot(act, wd_ref[...], preferred_element_type=F32)

    @pl.when(valid & (s == 0))
    def _():
        @pl.when(i == 0)
        def _():
            gather_start(0)
            obuf[...] = jnp.zeros(obuf.shape, obuf.dtype)

        gather_wait()
        x = _rows_2d(xbuf, x2_scr)
        hbuf[...] = _rms(x, gain_ref[...]).astype(BF16)

        nxt = jnp.minimum(i + 1, n_tiles - 1) * MOE_TILE
        prv = jnp.maximum(i - 1, 0) * MOE_TILE
        for r in range(MOE_TILE):
            tok = tok_ref[nxt + r]
            src = jnp.where(tok < n_tokens, tok, 0)
            pltpu.make_async_copy(x_hbm.at[pl.ds(src, 1)], xbuf.at[pl.ds(r, 1)], sem_in).start(priority=1)
            dst = jnp.where(i > 0, tok_ref[prv + r], n_tokens + r)
            pltpu.make_async_copy(obuf.at[pl.ds(r, 1)], out_hbm.at[pl.ds(dst, 1)], sem_out).start(priority=r % 2)

        x2_scr[...] = x2_scr[...] + expert_mlp()

    @pl.when(valid & (s == 1))
    def _():
        @pl.when(jnp.logical_not(next_valid))
        def _():
            gather_wait()

        y = expert_mlp()
        scatter_wait()
        obuf[...] = (x2_scr[...] + y).reshape(obuf.shape)

        @pl.when(jnp.logical_not(next_valid))
        def _():
            scatter_start(i)
            scatter_wait()


def _grouped_moe(x3, route, gain, wg_bf16, wu_bf16, wd_bf16):
    t, _, d = x3.shape
    f = wg_bf16.shape[2]
    n_tiles = t // MOE_TILE + N_BUCKETS
    step_expert, tile_valid, token_of, gates = _moe_plan(route, n_tiles)
    grid_spec = pltpu.PrefetchScalarGridSpec(
        num_scalar_prefetch=3,
        grid=(n_tiles, 2),
        in_specs=[pl.BlockSpec(memory_space=pl.ANY),
                  pl.BlockSpec((MOE_TILE, 2), lambda i, s, se, tv, tok: (i, 0)),
                  pl.BlockSpec((1, d), lambda i, s, se, tv, tok: (0, 0)),
                  pl.BlockSpec((None, d, f), lambda i, s, se, tv, tok: (se[2 * i + s], 0, 0)),
                  pl.BlockSpec((None, d, f), lambda i, s, se, tv, tok: (se[2 * i + s], 0, 0)),
                  pl.BlockSpec((None, f, d), lambda i, s, se, tv, tok: (se[2 * i + s], 0, 0))],
        out_specs=pl.BlockSpec(memory_space=pl.ANY),
        scratch_shapes=[pltpu.VMEM((MOE_TILE, 1, d), F32),
                        pltpu.VMEM((MOE_TILE, d), F32),
                        pltpu.VMEM((MOE_TILE, d), BF16),
                        pltpu.VMEM((MOE_TILE, 1, d), F32),
                        pltpu.SemaphoreType.DMA(()),
                        pltpu.SemaphoreType.DMA(())],
    )
    return pl.pallas_call(
        _moe_kernel,
        grid_spec=grid_spec,
        out_shape=jax.ShapeDtypeStruct((t + MOE_TILE, 1, d), F32),
        compiler_params=_cparams("arbitrary", "arbitrary"),
        name="moe_experts",
    )(step_expert, tile_valid, token_of, x3, gates, gain[None, :], wg_bf16, wu_bf16, wd_bf16)


def kernel(x, positions, norm_mix, norm_ffn, a_w_in, a_q_norm, a_k_norm, b_w_group, b_scale, ab_w_out,
           c_w_in, c_v_norm, c_w_spatial, c_b_spatial, c_w_out, router_w, router_bias,
           expert_w_gate, expert_w_up, expert_w_down):
    batch, t, d = x.shape
    assert batch == 1 and t % (N_STREAM * ATTN_BLOCK) == 0
    depth = norm_mix.shape[0]
    a_width = (a_w_in.shape[2] - b_w_group.shape[1] * b_w_group.shape[2]) // 3
    n_heads = a_width // HEAD_DIM
    ns = t // N_STREAM

    pos_rows = positions[0].reshape(ns, N_STREAM).T.reshape(t)
    cos, sin = (tab.reshape(N_STREAM, ns, HEAD_DIM) for tab in _trig_tables(pos_rows))
    router_w_pad = jnp.pad(router_w, ((0, 0), (0, HEAD_DIM - N_EXPERTS)))
    router_hi = router_w_pad.astype(BF16)
    router_lo = (router_w_pad - router_hi.astype(F32)).astype(BF16)
    a_w_in, b_w_group, ab_w_out, c_w_in, c_w_out = (
        w.astype(BF16) for w in (a_w_in, b_w_group, ab_w_out, c_w_in, c_w_out))
    experts = (expert_w_gate, expert_w_up, expert_w_down)
    n_experts, _, d_expert = expert_w_gate.shape[1:]

    assert depth % 2 == 0

    xs = x[0]
    for layer in range(depth):
        i = layer // 2
        router = (norm_ffn[layer][None, :], router_hi, router_lo, router_bias[:, None])
        if layer % 2 == 0:
            proj3, *w_this = _even_in_proj(xs, t, norm_mix[layer], a_w_in, i,
                                           jnp.stack([a_q_norm[i], a_k_norm[i]]), cos, sin, a_width,
                                           experts, layer)
            oa3, *w_next = _dilated_attention(proj3, n_heads, experts, layer + 1)
            ob3 = _pooling_mixer(proj3, b_w_group, i, b_scale[i])
            x3, route = _even_out_proj(oa3, ob3, ab_w_out, i, xs, t, router)
            wg, wu, wd = w_this
        else:
            z = _odd_in_proj(xs, t, norm_mix[layer], c_w_in, i)
            x3, route = _odd_out_proj(z, c_v_norm[i], c_w_spatial[i], c_b_spatial[i], c_w_out, i, xs, t, router)
            wg, wu, wd = w_next
        xs = _grouped_moe(x3, route, norm_ffn[layer],
                          wg.reshape(n_experts, d, d_expert), wu.reshape(n_experts, d, d_expert),
                          wd.reshape(n_experts, d_expert, d))
    return xs[:t].reshape(1, t, d)
```
